```python
import jax, jax.numpy as jnp
from jax import lax
import numpy as np

D_MODEL = 1024
BATCH = 4
SEQ = 4096
DEPTH = 4
DEC_BATCH = 128
DEC_SEQ = 8
PAST_LEN = 2048
PAGE_SIZE = 128

HEAD_DIM = 64
DIL_PATTERNS = ((128, 1), (512, 4), (2048, 16))
N_DIL = len(DIL_PATTERNS)
SPAN = 128
A_HEADS = 8
A_WIDTH = A_HEADS * HEAD_DIM
POOL_WINDOWS = (2, 4, 8, 16)
N_POOL = len(POOL_WINDOWS)
D_POOL = D_MODEL // 2
POOL_GC = D_POOL // N_POOL
POOL_STATE = max(POOL_WINDOWS) - 1
AB_IN = N_DIL * 3 * A_WIDTH + D_POOL
AB_OUT = A_WIDTH + D_POOL
CONV_W = 3
D_FF = 2816
ROPE_THETA = 10000.0
EPS = 1e-6
N_AB = (DEPTH + 1) // 2
N_C = DEPTH // 2

kernel_name = "hybrid_dilated_pool_conv_macaron_step"

F32 = jnp.float32


def rms_norm(x, g):
    xf = x.astype(F32)
    y = xf * lax.rsqrt(jnp.mean(xf * xf, axis=-1, keepdims=True) + EPS)
    return (y * g.astype(F32)).astype(x.dtype)


def swiglu(x, w_gu, w_down):
    g, u = jnp.split(x @ w_gu, 2, axis=-1)
    return (jax.nn.silu(g) * u) @ w_down


def rope(x, pos):
    half = HEAD_DIM // 2
    inv = jnp.power(ROPE_THETA, -jnp.arange(half, dtype=F32) / half)
    ang = pos.astype(F32)[:, None] * inv[None, :]
    shape = (1, ang.shape[0]) + (1,) * (x.ndim - 3) + (half,)
    cos = jnp.cos(ang).reshape(shape)
    sin = jnp.sin(ang).reshape(shape)
    xf = x.astype(F32)
    x1, x2 = xf[..., :half], xf[..., half:]
    return jnp.concatenate([x1 * cos - x2 * sin, x2 * cos + x1 * sin], axis=-1).astype(x.dtype)


def dilated_attn_prompt(q, k, v, dil):
    Bn, S, H, Dh = q.shape
    n_sub = S // dil
    n_blk = -(-n_sub // SPAN)
    pad = n_blk * SPAN - n_sub

    def to_sub(t):
        t = jnp.moveaxis(t.reshape(Bn, n_sub, dil, H, Dh), 2, 1)
        t = jnp.pad(t, ((0, 0), (0, 0), (0, pad), (0, 0), (0, 0)))
        return t.reshape(Bn, dil, n_blk, SPAN, H, Dh)

    def with_prev(t):
        prev = jnp.pad(t[:, :, :-1], ((0, 0), (0, 0), (1, 0), (0, 0), (0, 0), (0, 0)))
        return jnp.concatenate([prev, t], axis=3)

    qs = to_sub(q.astype(F32)) * (Dh ** -0.5)
    kb = with_prev(to_sub(k.astype(F32)))
    vb = with_prev(to_sub(v.astype(F32)))
    s = jnp.einsum('brnqhd,brnkhd->brnhqk', qs, kb)
    qi = jnp.arange(SPAN)[:, None]
    kj = jnp.arange(2 * SPAN)[None, :]
    band = (kj >= qi) & (kj <= qi + SPAN)
    has_prev = (jnp.arange(n_blk) > 0)[:, None, None] | (kj >= SPAN)[None]
    mask = band[None] & has_prev
    s = jnp.where(mask[None, None, :, None], s, -jnp.inf)
    m = jnp.max(s, axis=-1, keepdims=True)
    p = jnp.exp(s - m)
    den = jnp.sum(p, axis=-1, keepdims=True)
    lse = jnp.swapaxes((m + jnp.log(den))[..., 0], 3, 4)
    o = jnp.einsum('brnhqk,brnkhd->brnqhd', p / den, vb)

    def from_sub(t):
        rest = t.shape[4:]
        t = t.reshape((Bn, dil, n_blk * SPAN) + rest)[:, :, :n_sub]
        return jnp.moveaxis(t, 1, 2).reshape((Bn, S) + rest)

    return from_sub(o), from_sub(lse)


def dilated_attn_sample(q, kv_new, kv_buf, dil):
    L = kv_buf.shape[1]
    T, Dh = q.shape[1], q.shape[-1]
    kvc = jnp.concatenate([kv_buf.astype(kv_new.dtype), kv_new], axis=1)
    idx = L + jnp.arange(T)[:, None] - dil * jnp.arange(SPAN + 1)[None, :]
    valid = idx >= 0
    kg = kvc[:, jnp.maximum(idx, 0)].astype(F32)
    s = jnp.einsum('bthd,btkhd->bthk', q.astype(F32) * (Dh ** -0.5), kg[:, :, :, 0])
    s = jnp.where(valid[None, :, None, :], s, -jnp.inf)
    m = jnp.max(s, axis=-1, keepdims=True)
    p = jnp.exp(s - m)
    den = jnp.sum(p, axis=-1, keepdims=True)
    lse = (m + jnp.log(den))[..., 0]
    o = jnp.einsum('bthk,btkhd->bthd', p / den, kg[:, :, :, 1])
    return o, lse


def pool_mixer(u, prev, pos0, pos, pool_w, pool_scale):
    Bn, S, C = u.shape
    P = prev.shape[1]
    ext = jnp.concatenate([prev.astype(u.dtype), u], axis=1)
    ext_pos = pos0 - P + jnp.arange(P + S)
    extf = jnp.where((ext_pos >= 0)[None, :, None], ext.astype(F32), 0.0)
    cs = jnp.concatenate([jnp.zeros((Bn, 1, C), F32), lax.cumsum(extf, axis=1)], axis=1)
    hi = cs[:, P + 1:]
    uf = u.astype(F32)
    outs = []
    for gi, w in enumerate(POOL_WINDOWS):
        sl = slice(gi * POOL_GC, (gi + 1) * POOL_GC)
        lo = cs[:, P + 1 - w:P + 1 - w + S, sl]
        cnt = jnp.minimum(w, pos + 1).astype(F32)[None, :, None]
        outs.append((hi[..., sl] - lo) / cnt - uf[..., sl])
    d = jnp.stack(outs, axis=2)
    y = jnp.einsum('bsgc,gcd->bsgd', d, pool_w.astype(F32)).reshape(Bn, S, C)
    y = y * pool_scale.astype(F32)
    return y.astype(u.dtype), ext[:, -P:]


def ab_mixer(h, pos, pos0, past_kv, prev_pool, w_in, w_out, pool_w, pool_scale):
    Bn, S, _ = h.shape
    z = h @ w_in
    qkv = z[..., :N_DIL * 3 * A_WIDTH].reshape(Bn, S, N_DIL, 3, A_HEADS, HEAD_DIM)
    qk = rope(qkv[:, :, :, :2], pos)
    outs, lses, rows = [], [], []
    for g, (win, dil) in enumerate(DIL_PATTERNS):
        q, k, v = qk[:, :, g, 0], qk[:, :, g, 1], qkv[:, :, g, 2]
        kv = jnp.stack([k, v], axis=2)
        if past_kv is None:
            o, lse = dilated_attn_prompt(q, k, v, dil)
            rows.append(kv[:, -min(win, S):])
        else:
            o, lse = dilated_attn_sample(q, kv, past_kv[g], dil)
            rows.append(kv)
        outs.append(o)
        lses.append(lse)
    wts = jax.nn.softmax(jnp.stack(lses, axis=0), axis=0)[..., None]
    a = jnp.sum(wts * jnp.stack(outs, axis=0), axis=0).reshape(Bn, S, A_WIDTH).astype(h.dtype)
    p, pool_rows = pool_mixer(z[..., N_DIL * 3 * A_WIDTH:], prev_pool, pos0, pos, pool_w, pool_scale)
    return jnp.concatenate([a, p], axis=-1) @ w_out, rows, pool_rows


def conv_mixer(h, prev, w_in, conv_w, w_out):
    S = h.shape[1]
    gb, gc, v = jnp.split(h @ w_in, 3, axis=-1)
    u = gc * v
    ext = jnp.concatenate([prev.astype(u.dtype), u], axis=1)
    y = conv_w[0] * ext[:, 0:S]
    for t in range(1, CONV_W):
        y = y + conv_w[t] * ext[:, t:t + S]
    return (gb * y) @ w_out, ext[:, -(CONV_W - 1):]


def run_trunk(x, pos0, win_caches, pool_state, conv_state,
              ffn1_norm, ffn1_w_gu, ffn1_w_down, mix_norm, ffn2_norm, ffn2_w_gu, ffn2_w_down,
              ab_w_in, ab_w_out, pool_w, pool_scale, conv_w_in, conv_w, conv_w_out, final_norm):
    Bn, S, _ = x.shape
    pos = pos0 + jnp.arange(S, dtype=jnp.int32)
    new_win = [[] for _ in range(N_DIL)]
    new_pool, new_conv = [], []
    for l in range(DEPTH):
        x = x + 0.5 * swiglu(rms_norm(x, ffn1_norm[l]), ffn1_w_gu[l], ffn1_w_down[l])
        h = rms_norm(x, mix_norm[l])
        j = l // 2
        if l % 2 == 0:
            past = None if win_caches is None else tuple(c[j] for c in win_caches)
            prev = jnp.zeros((Bn, POOL_STATE, D_POOL), x.dtype) if pool_state is None else pool_state[j]
            mix, rows, prows = ab_mixer(h, pos, pos0, past, prev, ab_w_in[j], ab_w_out[j], pool_w[j], pool_scale[j])
            for g in range(N_DIL):
                new_win[g].append(rows[g])
            new_pool.append(prows)
        else:
            prev = jnp.zeros((Bn, CONV_W - 1, D_MODEL), x.dtype) if conv_state is None else conv_state[j]
            mix, crow = conv_mixer(h, prev, conv_w_in[j], conv_w[j], conv_w_out[j])
            new_conv.append(crow)
        x = x + mix
        x = x + 0.5 * swiglu(rms_norm(x, ffn2_norm[l]), ffn2_w_gu[l], ffn2_w_down[l])
    y = rms_norm(x, final_norm)
    wins = [jnp.stack(new_win[g], axis=0) for g in range(N_DIL)]
    return y, wins, jnp.stack(new_pool, axis=0), jnp.stack(new_conv, axis=0)


def setup_inputs(seed: int = 0) -> dict:
    key = jax.random.key(seed)
    ks = iter(jax.random.split(key, 32))

    def nrm(shape, scale):
        return jax.random.normal(next(ks), shape, F32) * scale

    def gain(shape):
        return 1.0 + nrm(shape, 0.02)

    d = D_MODEL
    return {
        "x_prompt": nrm((BATCH, SEQ, d), 1.0),
        "x_sample": nrm((DEC_BATCH, DEC_SEQ, d), 1.0),
        "cache_win0": nrm((N_AB, DEC_BATCH, min(DIL_PATTERNS[0][0], PAST_LEN), 2, A_HEADS, HEAD_DIM), 1.0),
        "cache_win1": nrm((N_AB, DEC_BATCH, min(DIL_PATTERNS[1][0], PAST_LEN), 2, A_HEADS, HEAD_DIM), 1.0),
        "cache_win2": nrm((N_AB, DEC_BATCH, min(DIL_PATTERNS[2][0], PAST_LEN), 2, A_HEADS, HEAD_DIM), 1.0),
        "state_pool": nrm((N_AB, DEC_BATCH, POOL_STATE, D_POOL), 1.0),
        "state_conv": nrm((N_C, DEC_BATCH, CONV_W - 1, d), 1.0),
        "ffn1_norm": gain((DEPTH, d)),
        "ffn1_w_gu": nrm((DEPTH, d, 2 * D_FF), d ** -0.5),
        "ffn1_w_down": nrm((DEPTH, D_FF, d), D_FF ** -0.5),
        "mix_norm": gain((DEPTH, d)),
        "ffn2_norm": gain((DEPTH, d)),
        "ffn2_w_gu": nrm((DEPTH, d, 2 * D_FF), d ** -0.5),
        "ffn2_w_down": nrm((DEPTH, D_FF, d), D_FF ** -0.5),
        "ab_w_in": nrm((N_AB, d, AB_IN), d ** -0.5),
        "ab_w_out": nrm((N_AB, AB_OUT, d), AB_OUT ** -0.5),
        "pool_w": nrm((N_AB, N_POOL, POOL_GC, POOL_GC), POOL_GC ** -0.5),
        "pool_scale": gain((N_AB, D_POOL)),
        "conv_w_in": nrm((N_C, d, 3 * d), d ** -0.5),
        "conv_w": nrm((N_C, CONV_W, d), CONV_W ** -0.5),
        "conv_w_out": nrm((N_C, d, d), d ** -0.5),
        "final_norm": gain((d,)),
    }


def reference(x_prompt, x_sample, cache_win0, cache_win1, cache_win2, state_pool, state_conv,
              ffn1_norm, ffn1_w_gu, ffn1_w_down, mix_norm, ffn2_norm, ffn2_w_gu, ffn2_w_down,
              ab_w_in, ab_w_out, pool_w, pool_scale, conv_w_in, conv_w, conv_w_out, final_norm):
    weights = (ffn1_norm, ffn1_w_gu, ffn1_w_down, mix_norm, ffn2_norm, ffn2_w_gu, ffn2_w_down,
               ab_w_in, ab_w_out, pool_w, pool_scale, conv_w_in, conv_w, conv_w_out, final_norm)
    y_prompt, win_p, pool_p, conv_p = run_trunk(x_prompt, 0, None, None, None, *weights)
    y_sample, win_s, pool_s, conv_s = run_trunk(x_sample, PAST_LEN, (cache_win0, cache_win1, cache_win2),
                                                state_pool, state_conv, *weights)
    return (y_prompt, y_sample, win_p[0], win_p[1], win_p[2], pool_p, conv_p,
            win_s[0], win_s[1], win_s[2], pool_s, conv_s)
```

```python
import functools

import jax
import jax.numpy as jnp
from jax import lax
from jax.experimental import pallas as pl
from jax.experimental.pallas import tpu as pltpu

F32 = jnp.float32
BF16 = jnp.bfloat16

D_MODEL = 1024
HEAD_DIM = 64
A_HEADS = 8
A_WIDTH = A_HEADS * HEAD_DIM
DIL_PATTERNS = ((128, 1), (512, 4), (2048, 16))
N_DIL = len(DIL_PATTERNS)
SPAN = 128
GROUP_W = 3 * A_WIDTH
QKV_W = N_DIL * GROUP_W
POOL_WINDOWS = (2, 4, 8, 16)
D_POOL = 512
POOL_GC = 128
POOL_STATE = 15
POOL_HALO = 16
CONV_W = 3
CONV_HALO = 8
D_FF = 2816
ROPE_THETA = 10000.0
EPS = 1e-6
PAST_LEN = 2048

V7X_LANES = 128
V7X_SUBLANES = 8
V7X_MXU_DIM = 256
V7X_VMEM_BYTES = 64 * 1024 * 1024
V7X_VMEM_USABLE = 56 * 1024 * 1024
MIB = 1024 * 1024

ROW_TILE = 512
ATT_TQ = 256
ATT_BLK = SPAN
ATT_S_NS = 2
NEG_INF = float("-inf")


def _vmem_limit(block_bytes):
    return int(min(V7X_VMEM_USABLE, block_bytes * 1.25 + 12 * MIB))


def _nbytes(shape, dtype):
    n = 1
    for s in shape:
        n *= s
    return n * jnp.dtype(dtype).itemsize


def _cparams(n_axes, block_bytes):
    return pltpu.CompilerParams(dimension_semantics=("arbitrary",) * n_axes,
                                vmem_limit_bytes=_vmem_limit(block_bytes))


def _resident(shape):
    nd = len(shape)
    return pl.BlockSpec(shape, lambda *_: (0,) * nd, pipeline_mode=pl.Buffered(1))


def _rms(x, w):
    return x * lax.rsqrt(jnp.mean(x * x, axis=-1, keepdims=True) + EPS) * w


def _dot(a, b):
    return jnp.dot(a, b, preferred_element_type=F32)


def _dot_nt(a, b):
    return lax.dot_general(a, b, (((1,), (1,)), ((), ())), preferred_element_type=F32)


def _ffn_kernel(x_ref, nw_ref, wgu_ref, wd_ref, *rest, chunks, final):
    if final:
        fw_ref, o_ref, a_scr = rest
    else:
        o_ref, a_scr = rest
    x = x_ref[...]
    h = _rms(x, nw_ref[...]).astype(BF16)
    for lo, width in chunks:
        g = _dot(h, wgu_ref[:, lo:lo + width])
        u = _dot(h, wgu_ref[:, D_FF + lo:D_FF + lo + width])
        a_scr[:, lo:lo + width] = (g * jax.nn.sigmoid(g) * u).astype(BF16)
    y = x + 0.5 * _dot(a_scr[...], wd_ref[...])
    if final:
        y = _rms(y, fw_ref[...])
    o_ref[...] = y


def _ffn(x, nw, wgu, wd, final_w=None):
    rows = x.shape[0]
    tm = ROW_TILE
    chunks = tuple((lo, V7X_MXU_DIM) for lo in range(0, D_FF, V7X_MXU_DIM))
    final = final_w is not None
    in_specs = [pl.BlockSpec((tm, D_MODEL), lambda i: (i, 0)),
                _resident((1, D_MODEL)), _resident((D_MODEL, 2 * D_FF)), _resident((D_FF, D_MODEL))]
    args = [x, nw.reshape(1, D_MODEL), wgu, wd]
    if final:
        in_specs.append(_resident((1, D_MODEL)))
        args.append(final_w.reshape(1, D_MODEL))
    vm = (4 * _nbytes((tm, D_MODEL), F32) + _nbytes((D_MODEL, 2 * D_FF), BF16) + _nbytes((D_FF, D_MODEL), BF16)
          + _nbytes((tm, D_FF), BF16))
    return pl.pallas_call(
        functools.partial(_ffn_kernel, chunks=chunks, final=final),
        grid=(rows // tm,),
        in_specs=in_specs,
        out_specs=pl.BlockSpec((tm, D_MODEL), lambda i: (i, 0)),
        out_shape=jax.ShapeDtypeStruct((rows, D_MODEL), F32),
        scratch_shapes=[pltpu.VMEM((tm, D_FF), BF16)],
        compiler_params=_cparams(1, vm),
        name="ffn_final" if final else "ffn",
    )(*args)


def _conv_kernel(*refs, nseq, seg, carry):
    if carry:
        x_ref, nw_ref, win_ref, cw_ref, wout_ref, o_ref, tail_ref, ext = refs
    else:
        x_ref, nw_ref, win_ref, cw_ref, wout_ref, halo_ref, o_ref, tail_ref, ext = refs
    tm = nseq * seg
    x = x_ref[...]
    h = _rms(x, nw_ref[...]).astype(BF16)
    gb = _dot(h, win_ref[:, 0:D_MODEL])
    gc = _dot(h, win_ref[:, D_MODEL:2 * D_MODEL])
    v = _dot(h, win_ref[:, 2 * D_MODEL:3 * D_MODEL])
    if carry:
        @pl.when(pl.program_id(1) == 0)
        def _():
            ext[:, 0:CONV_HALO, :] = jnp.zeros((nseq, CONV_HALO, D_MODEL), F32)
    else:
        ext[:, 0:CONV_HALO, :] = halo_ref[...]
    ext[:, CONV_HALO:CONV_HALO + seg, :] = (gc * v).reshape(nseq, seg, D_MODEL)
    cw = cw_ref[...]
    y = cw[0] * ext[:, CONV_HALO - 2:CONV_HALO - 2 + seg, :]
    for k in range(1, CONV_W):
        y = y + cw[k] * ext[:, CONV_HALO - 2 + k:CONV_HALO - 2 + k + seg, :]
    o_ref[...] = x + _dot((gb * y.reshape(tm, D_MODEL)).astype(BF16), wout_ref[...])
    tail = ext[:, seg:seg + CONV_HALO, :]
    tail_ref[...] = tail
    if carry:
        ext[:, 0:CONV_HALO, :] = tail


def _conv_layer(x, nw, w_in, cw, w_out, *, nb, nt, nseq, seg, halo=None):
    tm = nseq * seg
    carry = halo is None
    rows = x.shape[0]
    in_specs = [pl.BlockSpec((tm, D_MODEL), lambda b, t: (b * nt + t, 0)),
                _resident((1, D_MODEL)), _resident((D_MODEL, 3 * D_MODEL)), _resident((CONV_W, D_MODEL)),
                _resident((D_MODEL, D_MODEL))]
    args = [x, nw.reshape(1, D_MODEL), w_in, cw, w_out]
    if not carry:
        in_specs.append(pl.BlockSpec((nseq, CONV_HALO, D_MODEL), lambda b, t: (b, 0, 0)))
        args.append(halo)
    vm = (8 * _nbytes((tm, D_MODEL), F32) + _nbytes((D_MODEL, 4 * D_MODEL), BF16)
          + _nbytes((nseq, CONV_HALO + seg, D_MODEL), F32) + 4 * _nbytes((nseq, CONV_HALO, D_MODEL), F32))
    return pl.pallas_call(
        functools.partial(_conv_kernel, nseq=nseq, seg=seg, carry=carry),
        grid=(nb, nt),
        in_specs=in_specs,
        out_specs=[pl.BlockSpec((tm, D_MODEL), lambda b, t: (b * nt + t, 0)),
                   pl.BlockSpec((nseq, CONV_HALO, D_MODEL), lambda b, t: (b, 0, 0))],
        out_shape=[jax.ShapeDtypeStruct((rows, D_MODEL), F32),
                   jax.ShapeDtypeStruct((nb * nseq, CONV_HALO, D_MODEL), F32)],
        scratch_shapes=[pltpu.VMEM((nseq, CONV_HALO + seg, D_MODEL), F32)],
        compiler_params=_cparams(2, vm),
        name="conv_layer",
    )(*args)


def _ab_in_kernel(*refs, nseq, seg, carry, pos0, win_first, win_rows):
    if carry:
        (x_ref, nw_ref, w_ref, wp_ref, cos_ref, sin_ref, pw_ref, ps_ref,
         qkv_ref, p_ref, win0_ref, win1_ref, win2_ref, tail_ref, ext) = refs
    else:
        (x_ref, nw_ref, w_ref, wp_ref, cos_ref, sin_ref, pw_ref, ps_ref, halo_ref,
         qkv_ref, p_ref, win0_ref, win1_ref, win2_ref, tail_ref, ext) = refs
    win_refs = (win0_ref, win1_ref, win2_ref)
    tm = nseq * seg
    t = pl.program_id(1)
    h = _rms(x_ref[...], nw_ref[...]).astype(BF16)

    cos = jnp.concatenate([cos_ref[...]] * (A_WIDTH // V7X_LANES), axis=1)
    sin = jnp.concatenate([sin_ref[...]] * (A_WIDTH // V7X_LANES), axis=1)
    lane = lax.broadcasted_iota(jnp.int32, (tm, A_WIDTH), 1)
    first_half = (lane % HEAD_DIM) < (HEAD_DIM // 2)

    def rope(z):
        partner = jnp.where(first_half, pltpu.roll(z, A_WIDTH - HEAD_DIM // 2, 1), pltpu.roll(z, HEAD_DIM // 2, 1))
        return z * cos + partner * sin

    for g in range(N_DIL):
        base = g * GROUP_W
        q = rope(_dot(h, w_ref[:, base:base + A_WIDTH])) * (HEAD_DIM ** -0.5)
        qkv_ref[:, base:base + A_WIDTH] = q.astype(BF16)
        k = rope(_dot(h, w_ref[:, base + A_WIDTH:base + 2 * A_WIDTH]))
        qkv_ref[:, base + A_WIDTH:base + 2 * A_WIDTH] = k.astype(BF16)
        v = _dot(h, w_ref[:, base + 2 * A_WIDTH:base + 3 * A_WIDTH])
        qkv_ref[:, base + 2 * A_WIDTH:base + 3 * A_WIDTH] = v.astype(BF16)

        def write_rows(g=g, k=k, v=v):
            r = win_rows[g]
            win_refs[g][:, 0:A_WIDTH] = k[tm - r:tm]
            win_refs[g][:, A_WIDTH:2 * A_WIDTH] = v[tm - r:tm]

        if win_first[g] == 0:
            write_rows()
        else:
            pl.when(t >= win_first[g])(write_rows)

    u = _dot(h, wp_ref[...])
    if carry:
        @pl.when(t == 0)
        def _():
            ext[:, 0:POOL_HALO, :] = jnp.zeros((nseq, POOL_HALO, D_POOL), F32)
        pos = pos0 + t * seg + lax.broadcasted_iota(jnp.int32, (1, seg, 1), 1)
    else:
        ext[:, 0:POOL_HALO, :] = halo_ref[...]
        pos = pos0 + lax.broadcasted_iota(jnp.int32, (1, seg, 1), 1)
    ext[:, POOL_HALO:POOL_HALO + seg, :] = u.reshape(nseq, seg, D_POOL)
    for gi, w in enumerate(POOL_WINDOWS):
        sl = slice(gi * POOL_GC, (gi + 1) * POOL_GC)
        tok = ext[:, POOL_HALO:POOL_HALO + seg, sl]
        acc = tok
        for k in range(1, w):
            acc = acc + ext[:, POOL_HALO - k:POOL_HALO - k + seg, sl]
        cnt = jnp.minimum(w, pos + 1).astype(F32)
        d = (acc / cnt - tok).reshape(tm, POOL_GC)
        p_ref[:, sl] = (_dot(d.astype(BF16), pw_ref[gi]) * ps_ref[:, sl]).astype(BF16)
    tail = ext[:, seg:seg + POOL_HALO, :]
    tail_ref[...] = tail
    if carry:
        ext[:, 0:POOL_HALO, :] = tail


def _ab_in(x, nw, w_qkv, w_pool, cos, sin, pool_w, pool_scale, *, nb, nt, nseq, seg, pos0, win_tot, halo=None):
    tm = nseq * seg
    carry = halo is None
    rows = x.shape[0]
    win_rows = tuple(min(w, tm) for w in win_tot)
    win_first = tuple(nt - max(w // tm, 1) for w in win_tot)
    cos_idx = (lambda b, t: (t, 0)) if carry else (lambda b, t: (0, 0))
    in_specs = [pl.BlockSpec((tm, D_MODEL), lambda b, t: (b * nt + t, 0)),
                _resident((1, D_MODEL)), _resident((D_MODEL, QKV_W)), _resident((D_MODEL, D_POOL)),
                pl.BlockSpec((tm, V7X_LANES), cos_idx), pl.BlockSpec((tm, V7X_LANES), cos_idx),
                _resident((len(POOL_WINDOWS), POOL_GC, POOL_GC)), _resident((1, D_POOL))]
    args = [x, nw.reshape(1, D_MODEL), w_qkv, w_pool, cos, sin, pool_w, pool_scale.reshape(1, D_POOL)]
    if not carry:
        in_specs.append(pl.BlockSpec((nseq, POOL_HALO, D_POOL), lambda b, t: (b, 0, 0)))
        args.append(halo)

    def win_spec(g):
        first, r = win_first[g], win_rows[g]
        return pl.BlockSpec((None, r, 2 * A_WIDTH), lambda b, t: (b, jnp.maximum(t - first, 0), 0))

    vm = (2 * _nbytes((tm, D_MODEL), F32) + _nbytes((D_MODEL, AB_IN_W), BF16) + 2 * _nbytes((tm, QKV_W), BF16)
          + 2 * sum(_nbytes((r, 2 * A_WIDTH), F32) for r in win_rows) + 4 * _nbytes((tm, D_POOL), F32)
          + _nbytes((nseq, POOL_HALO + seg, D_POOL), F32) + 6 * _nbytes((tm, A_WIDTH), F32))
    return pl.pallas_call(
        functools.partial(_ab_in_kernel, nseq=nseq, seg=seg, carry=carry, pos0=pos0,
                          win_first=win_first, win_rows=win_rows),
        grid=(nb, nt),
        in_specs=in_specs,
        out_specs=[pl.BlockSpec((tm, QKV_W), lambda b, t: (b * nt + t, 0)),
                   pl.BlockSpec((tm, D_POOL), lambda b, t: (b * nt + t, 0)),
                   win_spec(0), win_spec(1), win_spec(2),
                   pl.BlockSpec((nseq, POOL_HALO, D_POOL), lambda b, t: (b, 0, 0))],
        out_shape=[jax.ShapeDtypeStruct((rows, QKV_W), BF16),
                   jax.ShapeDtypeStruct((rows, D_POOL), BF16)]
                  + [jax.ShapeDtypeStruct((nb, w, 2 * A_WIDTH), F32) for w in win_tot]
                  + [jax.ShapeDtypeStruct((nb * nseq, POOL_HALO, D_POOL), F32)],
        scratch_shapes=[pltpu.VMEM((nseq, POOL_HALO + seg, D_POOL), F32)],
        compiler_params=_cparams(2, vm),
        name="ab_in",
    )(*args)


AB_IN_W = QKV_W + D_POOL


def _attn_kernel(*refs, merge):
    if merge:
        q_ref, kc_ref, kp_ref, vc_ref, vp_ref, o4_ref, l4_ref, o16_ref, l16_ref, out_ref = refs
    else:
        q_ref, kc_ref, kp_ref, vc_ref, vp_ref, out_ref, lse_ref = refs
    n = pl.program_id(2)
    blk = ATT_BLK
    qi = lax.broadcasted_iota(jnp.int32, (blk, 2 * blk), 0)
    kj = lax.broadcasted_iota(jnp.int32, (blk, 2 * blk), 1)
    band = (kj >= qi) & (kj <= qi + SPAN)
    band_first = band & ((kj >= blk) | (n > 0))
    lane = lax.broadcasted_iota(jnp.int32, (blk, V7X_LANES), 1)
    even_head = lane < HEAD_DIM
    lane_head = lane // (V7X_LANES // A_HEADS)

    for c in range(ATT_TQ // blk):
        rows = slice(c * blk, (c + 1) * blk)
        mask = band_first if c == 0 else band
        mask2 = jnp.concatenate([mask, mask], axis=0)
        lse_blk = jnp.zeros((blk, V7X_LANES), F32)
        for hp in range(A_HEADS // 2):
            ls = slice(hp * V7X_LANES, (hp + 1) * V7X_LANES)
            q2 = q_ref[rows, ls].astype(F32)
            qs = jnp.concatenate([jnp.where(even_head, q2, 0.0), jnp.where(even_head, 0.0, q2)], axis=0).astype(BF16)
            if c == 0:
                k2 = jnp.concatenate([kp_ref[:, ls], kc_ref[0:blk, ls]], axis=0)
                v2 = jnp.concatenate([vp_ref[:, ls], vc_ref[0:blk, ls]], axis=0)
            else:
                k2 = kc_ref[(c - 1) * blk:(c + 1) * blk, ls]
                v2 = vc_ref[(c - 1) * blk:(c + 1) * blk, ls]
            s = jnp.where(mask2, _dot_nt(qs, k2), NEG_INF)
            m = jnp.max(s, axis=-1, keepdims=True)
            p = jnp.exp(s - m)
            den = jnp.sum(p, axis=-1, keepdims=True)
            o = _dot(p.astype(BF16), v2) / den
            lse = m + jnp.log(den)
            o2 = jnp.where(even_head, o[0:blk], o[blk:2 * blk])
            lse_e, lse_o = lse[0:blk], lse[blk:2 * blk]
            if merge:
                l4 = l4_ref[rows, :]
                l16 = l16_ref[rows, :]
                stride = V7X_LANES // A_HEADS
                ce, co = 2 * hp * stride, (2 * hp + 1) * stride

                def weights(a, b4, b16):
                    mx = jnp.maximum(a, jnp.maximum(b4, b16))
                    ea, e4, e16 = jnp.exp(a - mx), jnp.exp(b4 - mx), jnp.exp(b16 - mx)
                    tot = ea + e4 + e16
                    return ea / tot, e4 / tot, e16 / tot

                we = weights(lse_e, l4[:, ce:ce + 1], l16[:, ce:ce + 1])
                wo = weights(lse_o, l4[:, co:co + 1], l16[:, co:co + 1])
                w1, w4, w16 = (jnp.where(even_head, a, b) for a, b in zip(we, wo))
                a2 = w1 * o2 + w4 * o4_ref[rows, ls].astype(F32) + w16 * o16_ref[rows, ls].astype(F32)
                out_ref[rows, ls] = a2.astype(BF16)
            else:
                out_ref[rows, ls] = o2.astype(BF16)
                lse_blk = jnp.where(lane_head == 2 * hp, lse_e, jnp.where(lane_head == 2 * hp + 1, lse_o, lse_blk))
        if not merge:
            lse_ref[rows, :] = lse_blk


def _attn_group(qkv, g, nb, seq, others=None):
    dil = DIL_PATTERNS[g][1]
    sub = seq // dil
    tq, blk = ATT_TQ, ATT_BLK
    nq = sub // tq
    merge = others is not None
    col = lambda r, part: r * (QKV_W // A_WIDTH) + 3 * g + part
    qkv_v = qkv.reshape(nb, sub, dil * QKV_W)
    cur = lambda part: pl.BlockSpec((None, tq, A_WIDTH), lambda b, r, n: (b, n, col(r, part)))
    prev = lambda part: pl.BlockSpec((None, blk, A_WIDTH),
                                     lambda b, r, n: (b, jnp.maximum(n * (tq // blk) - 1, 0), col(r, part)))
    in_specs = [cur(0), cur(1), prev(1), cur(2), prev(2)]
    args = [qkv_v, qkv_v, qkv_v, qkv_v, qkv_v]
    o_spec = pl.BlockSpec((None, tq, A_WIDTH), lambda b, r, n: (b, n, r))
    l_spec = pl.BlockSpec((None, tq, V7X_LANES), lambda b, r, n: (b, n, r))
    o_shape = jax.ShapeDtypeStruct((nb, sub, dil * A_WIDTH), BF16)
    if merge:
        assert dil == 1
        o4, l4, o16, l16 = others
        in_specs += [o_spec, l_spec, o_spec, l_spec]
        args += [o4.reshape(nb, seq, A_WIDTH), l4.reshape(nb, seq, V7X_LANES),
                 o16.reshape(nb, seq, A_WIDTH), l16.reshape(nb, seq, V7X_LANES)]
        out_specs, out_shape = o_spec, o_shape
    else:
        out_specs = [o_spec, l_spec]
        out_shape = [o_shape, jax.ShapeDtypeStruct((nb, sub, dil * V7X_LANES), F32)]
    vm = 2 * (6 * _nbytes((tq, A_WIDTH), BF16) + 2 * _nbytes((tq, V7X_LANES), F32) + 2 * _nbytes((tq, A_WIDTH), BF16))
    res = pl.pallas_call(
        functools.partial(_attn_kernel, merge=merge),
        grid=(nb, dil, nq),
        in_specs=in_specs,
        out_specs=out_specs,
        out_shape=out_shape,
        compiler_params=_cparams(3, vm),
        name=f"attn_d{dil}",
    )(*args)
    if merge:
        return res.reshape(nb * seq, A_WIDTH)
    return res[0].reshape(nb * seq, A_WIDTH), res[1].reshape(nb * seq, V7X_LANES)


def _attn_s_kernel(qkv_ref, c0_ref, c1_ref, c2_ref, out_ref, *, ns, seg):
    hq = A_HEADS * seg
    row = lax.broadcasted_iota(jnp.int32, (hq, A_WIDTH), 0)
    lane = lax.broadcasted_iota(jnp.int32, (hq, A_WIDTH), 1)
    head_lanes = (lane // HEAD_DIM) == (row // seg)
    out_lane = lax.broadcasted_iota(jnp.int32, (seg, A_WIDTH), 1) // HEAD_DIM
    pad = V7X_LANES - seg
    outs = []
    for s in range(ns):
        new = qkv_ref[s * seg:(s + 1) * seg, :].astype(F32)
        scores, values = [], []
        for g, (win, dil) in enumerate(DIL_PATTERNS):
            base = g * GROUP_W
            q = new[:, base:base + A_WIDTH]
            qrows = jnp.where(head_lanes, jnp.concatenate([q] * A_HEADS, axis=0), 0.0).astype(BF16)
            if g == 0:
                cache = c0_ref[s]
            elif g == 1:
                cache = c1_ref[s]
            else:
                cache = c2_ref[s].reshape(c2_ref.shape[1] * c2_ref.shape[2], 2 * A_WIDTH)
            nk = cache.shape[0]
            tok = lax.broadcasted_iota(jnp.int32, (hq, nk), 0) % seg
            key = lax.broadcasted_iota(jnp.int32, (hq, nk), 1)
            if g == 2:
                valid = (key % seg) == tok
            else:
                valid = ((key % dil) == (tok % dil)) & (key >= tok)
            scores.append(jnp.where(valid, _dot_nt(qrows, cache[:, 0:A_WIDTH].astype(BF16)), NEG_INF))
            values.append(cache[:, A_WIDTH:2 * A_WIDTH].astype(BF16))
            kn = jnp.concatenate([new[:, base + A_WIDTH:base + 2 * A_WIDTH], jnp.zeros((pad, A_WIDTH), F32)], axis=0)
            vn = jnp.concatenate([new[:, base + 2 * A_WIDTH:base + 3 * A_WIDTH], jnp.zeros((pad, A_WIDTH), F32)], axis=0)
            tok_n = lax.broadcasted_iota(jnp.int32, (hq, V7X_LANES), 0) % seg
            key_n = lax.broadcasted_iota(jnp.int32, (hq, V7X_LANES), 1)
            valid_n = (key_n <= tok_n) & (((tok_n - key_n) % dil) == 0)
            scores.append(jnp.where(valid_n, _dot_nt(qrows, kn.astype(BF16)), NEG_INF))
            values.append(vn.astype(BF16))
        m = scores[0].max(axis=-1, keepdims=True)
        for sc in scores[1:]:
            m = jnp.maximum(m, sc.max(axis=-1, keepdims=True))
        den = jnp.zeros((hq, 1), F32)
        acc = jnp.zeros((hq, A_WIDTH), F32)
        for sc, val in zip(scores, values):
            p = jnp.exp(sc - m)
            den = den + p.sum(axis=-1, keepdims=True)
            acc = acc + _dot(p.astype(BF16), val)
        acc = acc / den
        o = jnp.zeros((seg, A_WIDTH), F32)
        for hh in range(A_HEADS):
            o = jnp.where(out_lane == hh, acc[hh * seg:(hh + 1) * seg], o)
        outs.append(o)
    out_ref[...] = jnp.concatenate(outs, axis=0).astype(BF16)


def _attn_sample(qkv, caches, j, nseq_total, seg):
    ns = ATT_S_NS
    c0, c1, c2 = caches
    kv = 2 * A_WIDTH
    c0v = c0.reshape(c0.shape[0], nseq_total, c0.shape[2], kv)
    c1v = c1.reshape(c1.shape[0], nseq_total, c1.shape[2], kv)
    d2 = DIL_PATTERNS[2][1]
    n2 = c2.shape[2] // d2
    c2v = c2.reshape(c2.shape[0], nseq_total, n2, d2, kv)
    vm = 2 * (_nbytes((ns * seg, QKV_W), BF16) + _nbytes((ns, c0.shape[2], kv), F32) + _nbytes((ns, c1.shape[2], kv), F32)
              + _nbytes((ns, n2, seg, kv), F32) + _nbytes((ns * seg, A_WIDTH), BF16)) + 8 * _nbytes((n2 * seg, A_WIDTH), F32)
    return pl.pallas_call(
        functools.partial(_attn_s_kernel, ns=ns, seg=seg),
        grid=(nseq_total // ns,),
        in_specs=[pl.BlockSpec((ns * seg, QKV_W), lambda i: (i, 0)),
                  pl.BlockSpec((None, ns, c0.shape[2], kv), lambda i: (j, i, 0, 0)),
                  pl.BlockSpec((None, ns, c1.shape[2], kv), lambda i: (j, i, 0, 0)),
                  pl.BlockSpec((None, ns, n2, seg, kv), lambda i: (j, i, 0, 0, 0))],
        out_specs=pl.BlockSpec((ns * seg, A_WIDTH), lambda i: (i, 0)),
        out_shape=jax.ShapeDtypeStruct((nseq_total * seg, A_WIDTH), BF16),
        compiler_params=_cparams(1, vm),
        name="attn_sample",
    )(qkv, c0v, c1v, c2v)


def _ab_out_kernel(x_ref, a_ref, p_ref, w_ref, o_ref):
    o_ref[...] = (x_ref[...] + _dot(a_ref[...], w_ref[0:A_WIDTH, :]) + _dot(p_ref[...], w_ref[A_WIDTH:A_WIDTH + D_POOL, :]))


def _ab_out(x, a, p, w_out):
    rows = x.shape[0]
    tm = ROW_TILE
    vm = 4 * _nbytes((tm, D_MODEL), F32) + 4 * _nbytes((tm, A_WIDTH), BF16) + _nbytes((D_MODEL, D_MODEL), BF16)
    return pl.pallas_call(
        _ab_out_kernel,
        grid=(rows // tm,),
        in_specs=[pl.BlockSpec((tm, D_MODEL), lambda i: (i, 0)), pl.BlockSpec((tm, A_WIDTH), lambda i: (i, 0)),
                  pl.BlockSpec((tm, D_POOL), lambda i: (i, 0)), _resident((A_WIDTH + D_POOL, D_MODEL))],
        out_specs=pl.BlockSpec((tm, D_MODEL), lambda i: (i, 0)),
        out_shape=jax.ShapeDtypeStruct((rows, D_MODEL), F32),
        compiler_params=_cparams(1, vm),
        name="ab_out",
    )(x, a, p, w_out)


def _rope_tables(pos):
    half = HEAD_DIM // 2
    inv = jnp.power(ROPE_THETA, -jnp.arange(half, dtype=F32) / half)
    ang = pos.astype(F32)[:, None] * inv[None, :]
    c, s = jnp.cos(ang), jnp.sin(ang)
    reps = V7X_LANES // HEAD_DIM
    return jnp.concatenate([c, c] * reps, axis=1), jnp.concatenate([-s, s] * reps, axis=1)


def _trunk(x, w, *, nb, seq, sample_state=None):
    tm = ROW_TILE
    sample = sample_state is not None
    depth = w["ffn1_norm"].shape[0]
    if sample:
        caches, pool_state, conv_state = sample_state
        nseq, seg = tm // seq, seq
        gnb, gnt = (nb * seq) // tm, 1
        pos0 = PAST_LEN
        cos, sin = _rope_tables(pos0 + jnp.arange(seq, dtype=jnp.int32))
        cos, sin = jnp.tile(cos, (nseq, 1)), jnp.tile(sin, (nseq, 1))
        win_tot = (tm,) * N_DIL
    else:
        nseq, seg = 1, tm
        gnb, gnt = nb, seq // tm
        pos0 = 0
        cos, sin = _rope_tables(jnp.arange(seq, dtype=jnp.int32))
        win_tot = tuple(min(win, seq) for win, _ in DIL_PATTERNS)
    wins = [[] for _ in range(N_DIL)]
    pools, convs = [], []
    for l in range(depth):
        j = l // 2
        x = _ffn(x, w["ffn1_norm"][l], w["ffn1_w_gu"][l], w["ffn1_w_down"][l])
        if l % 2 == 0:
            halo = None
            if sample:
                halo = jnp.pad(pool_state[j], ((0, 0), (POOL_HALO - POOL_STATE, 0), (0, 0)))
            qkv, p, w0, w1, w2, ptail = _ab_in(
                x, w["mix_norm"][l], w["ab_w_qkv"][j], w["ab_w_pool"][j], cos, sin, w["pool_w"][j], w["pool_scale"][j],
                nb=gnb, nt=gnt, nseq=nseq, seg=seg, pos0=pos0, win_tot=win_tot, halo=halo)
            if sample:
                a = _attn_sample(qkv, caches, j, nb, seq)
            else:
                o16, l16 = _attn_group(qkv, 2, nb, seq)
                o4, l4 = _attn_group(qkv, 1, nb, seq)
                a = _attn_group(qkv, 0, nb, seq, others=(o4, l4, o16, l16))
            x = _ab_out(x, a, p, w["ab_w_out"][j])
            for g, wg in enumerate((w0, w1, w2)):
                wins[g].append(wg.reshape(nb, -1, 2, A_HEADS, HEAD_DIM))
            pools.append(ptail[:, POOL_HALO - POOL_STATE:])
        else:
            halo = None
            if sample:
                halo = jnp.pad(conv_state[j], ((0, 0), (CONV_HALO - (CONV_W - 1), 0), (0, 0)))
            x, ctail = _conv_layer(x, w["mix_norm"][l], w["conv_w_in"][j], w["conv_w"][j], w["conv_w_out"][j],
                                   nb=gnb, nt=gnt, nseq=nseq, seg=seg, halo=halo)
            convs.append(ctail[:, CONV_HALO - (CONV_W - 1):])
        x = _ffn(x, w["ffn2_norm"][l], w["ffn2_w_gu"][l], w["ffn2_w_down"][l],
                 final_w=w["final_norm"] if l == depth - 1 else None)
    return x, [jnp.stack(ws, axis=0) for ws in wins], jnp.stack(pools, axis=0), jnp.stack(convs, axis=0)


def kernel(x_prompt, x_sample, cache_win0, cache_win1, cache_win2, state_pool, state_conv, ffn1_norm, ffn1_w_gu, ffn1_w_down, mix_norm, ffn2_norm, ffn2_w_gu, ffn2_w_down, ab_w_in, ab_w_out, pool_w, pool_scale, conv_w_in, conv_w, conv_w_out, final_norm):
    batch, seq, _ = x_prompt.shape
    dbatch, dseq, _ = x_sample.shape
    assert seq % ROW_TILE == 0 and ROW_TILE % dseq == 0 and (dbatch * dseq) % ROW_TILE == 0
    assert all((seq // dil) % ATT_TQ == 0 for _, dil in DIL_PATTERNS)
    assert dseq == V7X_SUBLANES and dseq <= DIL_PATTERNS[2][1] and PAST_LEN >= POOL_STATE
    assert all(c.shape[2] == win for c, (win, _) in zip((cache_win0, cache_win1, cache_win2), DIL_PATTERNS))
    w = dict(
        ffn1_norm=ffn1_norm, ffn1_w_gu=ffn1_w_gu.astype(BF16), ffn1_w_down=ffn1_w_down.astype(BF16),
        mix_norm=mix_norm, ffn2_norm=ffn2_norm, ffn2_w_gu=ffn2_w_gu.astype(BF16), ffn2_w_down=ffn2_w_down.astype(BF16),
        ab_w_qkv=ab_w_in[:, :, :QKV_W].astype(BF16), ab_w_pool=ab_w_in[:, :, QKV_W:].astype(BF16),
        ab_w_out=ab_w_out.astype(BF16), pool_w=pool_w.astype(BF16), pool_scale=pool_scale,
        conv_w_in=conv_w_in.astype(BF16), conv_w=conv_w, conv_w_out=conv_w_out.astype(BF16), final_norm=final_norm)
    y_p, win_p, pool_p, conv_p = _trunk(x_prompt.reshape(batch * seq, D_MODEL), w, nb=batch, seq=seq)
    y_s, win_s, pool_s, conv_s = _trunk(x_sample.reshape(dbatch * dseq, D_MODEL), w, nb=dbatch, seq=dseq,
                                        sample_state=((cache_win0, cache_win1, cache_win2), state_pool, state_conv))
    return (y_p.reshape(batch, seq, D_MODEL), y_s.reshape(dbatch, dseq, D_MODEL),
            win_p[0], win_p[1], win_p[2], pool_p, conv_p,
            win_s[0], win_s[1], win_s[2], pool_s, conv_s)
```

```python
import functools

import jax
import jax.numpy as jnp
from jax import lax
from jax.experimental import pallas as pl
from jax.experimental.pallas import tpu as pltpu

F32 = jnp.float32
BF16 = jnp.bfloat16

D_MODEL = 1024
HEAD_DIM = 64
A_HEADS = 8
A_WIDTH = A_HEADS * HEAD_DIM
DIL_PATTERNS = ((128, 1), (512, 4), (2048, 16))
N_DIL = len(DIL_PATTERNS)
SPAN = 128
GROUP_W = 3 * A_WIDTH
QKV_W = N_DIL * GROUP_W
KV_W = 2 * A_WIDTH
POOL_WINDOWS = (2, 4, 8, 16)
D_POOL = 512
POOL_GC = 128
POOL_STATE = 15
CONV_W = 3
D_FF = 2816
ROPE_THETA = 10000.0
EPS = 1e-6
PAST_LEN = 2048

V7X_LANES = 128
V7X_SUBLANES = 8
V7X_MXU_DIM = 256
V7X_VMEM_USABLE = 56 * 1024 * 1024
MIB = 1024 * 1024

ROW_TILE = 512
SAMPLE_ROW_TILE = 256
ATT_TQ = 256
ATT_BLK = SPAN
LSE_LANES = V7X_LANES // A_HEADS
LANE_CHUNKS = A_WIDTH // V7X_LANES
NEG_INF = float("-inf")


def _round_up(n, m):
    return -(-n // m) * m


def _nbytes(shape, dtype):
    n = 1
    for s in shape:
        n *= s
    return n * jnp.dtype(dtype).itemsize


def _cparams(n_axes, block_bytes):
    limit = int(min(V7X_VMEM_USABLE, block_bytes * 1.25 + 12 * MIB))
    return pltpu.CompilerParams(dimension_semantics=("arbitrary",) * n_axes, vmem_limit_bytes=limit)


def _layer_block(shape, layer):
    nd = len(shape)
    return pl.BlockSpec((None,) + tuple(shape), lambda *_: (layer,) + (0,) * nd, pipeline_mode=pl.Buffered(1))


def _const_block(shape):
    nd = len(shape)
    return pl.BlockSpec(tuple(shape), lambda *_: (0,) * nd, pipeline_mode=pl.Buffered(1))


def _rms(x, w):
    return x * lax.rsqrt(jnp.mean(x * x, axis=-1, keepdims=True) + EPS) * w


def _dot(a, b):
    return jnp.dot(a, b, preferred_element_type=F32)


def _dot_nt(a, b):
    return lax.dot_general(a, b, (((1,), (1,)), ((), ())), preferred_element_type=F32)


def _ffn_kernel(x_ref, nw_ref, wgu_ref, wd_ref, *rest, final):
    if final:
        fw_ref, o_ref, a_scr = rest
    else:
        o_ref, a_scr = rest
    x = x_ref[...]
    h = _rms(x, nw_ref[...]).astype(BF16)
    for lo in range(0, D_FF, V7X_MXU_DIM):
        g = _dot(h, wgu_ref[:, lo:lo + V7X_MXU_DIM])
        u = _dot(h, wgu_ref[:, D_FF + lo:D_FF + lo + V7X_MXU_DIM])
        a_scr[:, lo:lo + V7X_MXU_DIM] = (g * jax.nn.sigmoid(g) * u).astype(BF16)
    y = x + 0.5 * _dot(a_scr[...], wd_ref[...])
    if final:
        y = _rms(y, fw_ref[...])
    o_ref[...] = y


def _ffn(x, nw, wgu, wd, layer, final_w=None):
    rows = x.shape[0]
    tm = ROW_TILE
    final = final_w is not None
    in_specs = [pl.BlockSpec((tm, D_MODEL), lambda i: (i, 0)),
                _layer_block((1, D_MODEL), layer), _layer_block((D_MODEL, 2 * D_FF), layer),
                _layer_block((D_FF, D_MODEL), layer)]
    args = [x, nw, wgu, wd]
    if final:
        in_specs.append(_const_block((1, D_MODEL)))
        args.append(final_w.reshape(1, D_MODEL))
    vm = (4 * _nbytes((tm, D_MODEL), F32) + _nbytes((D_MODEL, 2 * D_FF), BF16) + _nbytes((D_FF, D_MODEL), BF16)
          + _nbytes((tm, D_FF), BF16))
    return pl.pallas_call(
        functools.partial(_ffn_kernel, final=final),
        grid=(rows // tm,),
        in_specs=in_specs,
        out_specs=pl.BlockSpec((tm, D_MODEL), lambda i: (i, 0)),
        out_shape=jax.ShapeDtypeStruct((rows, D_MODEL), F32),
        scratch_shapes=[pltpu.VMEM((tm, D_FF), BF16)],
        compiler_params=_cparams(1, vm),
        name="ffn_final" if final else "ffn",
    )(*args)


def _hist_rows(positions, stride):
    return _round_up(positions * stride, V7X_SUBLANES)


def _load_history(ext, halo_ref, cur, hist, tm):
    @pl.when(pl.program_id(1) == 0)
    def _():
        ext[0:hist, :] = halo_ref[...]
    ext[hist:hist + tm, :] = cur


def _carry_history(ext, hist, tm):
    ext[0:hist, :] = ext[tm:tm + hist, :]


def _conv_kernel(x_ref, nw_ref, win_ref, cw_ref, wout_ref, halo_ref, o_ref, tail_ref, ext, *, tm, stride, hist, tail_rows):
    x = x_ref[...]
    h = _rms(x, nw_ref[...]).astype(BF16)
    gb = _dot(h, win_ref[:, 0:D_MODEL])
    gc = _dot(h, win_ref[:, D_MODEL:2 * D_MODEL])
    v = _dot(h, win_ref[:, 2 * D_MODEL:3 * D_MODEL])
    _load_history(ext, halo_ref, gc * v, hist, tm)
    cw = cw_ref[...]
    y = cw[CONV_W - 1] * ext[hist:hist + tm, :]
    for back in range(1, CONV_W):
        lo = hist - back * stride
        y = y + cw[CONV_W - 1 - back] * ext[lo:lo + tm, :]
    o_ref[...] = x + _dot((gb * y).astype(BF16), wout_ref[...])
    tail_ref[...] = ext[hist + tm - tail_rows:hist + tm, :]
    _carry_history(ext, hist, tm)


def _tail_spec(tail_rows, width, nt, per_tile):
    if per_tile:
        return nt * tail_rows, pl.BlockSpec((None, tail_rows, width), lambda b, t: (b, t, 0))
    return tail_rows, pl.BlockSpec((None, tail_rows, width), lambda b, t: (b, 0, 0))


def _conv_layer(x, nw, w_in, cw, w_out, layer, halo, *, nb, nt, tm, stride, tail_rows, tail_per_tile):
    hist = _hist_rows(CONV_W - 1, stride)
    assert halo.shape == (nb, hist, D_MODEL)
    rows = x.shape[0]
    tail_total, tail_spec = _tail_spec(tail_rows, D_MODEL, nt, tail_per_tile)
    vm = (8 * _nbytes((tm, D_MODEL), F32) + _nbytes((D_MODEL, 4 * D_MODEL), BF16)
          + _nbytes((hist + tm, D_MODEL), F32) + 4 * _nbytes((hist + tail_rows, D_MODEL), F32))
    return pl.pallas_call(
        functools.partial(_conv_kernel, tm=tm, stride=stride, hist=hist, tail_rows=tail_rows),
        grid=(nb, nt),
        in_specs=[pl.BlockSpec((tm, D_MODEL), lambda b, t: (b * nt + t, 0)),
                  _layer_block((1, D_MODEL), layer), _layer_block((D_MODEL, 3 * D_MODEL), layer),
                  _layer_block((CONV_W, D_MODEL), layer), _layer_block((D_MODEL, D_MODEL), layer),
                  pl.BlockSpec((None, hist, D_MODEL), lambda b, t: (b, 0, 0))],
        out_specs=[pl.BlockSpec((tm, D_MODEL), lambda b, t: (b * nt + t, 0)), tail_spec],
        out_shape=[jax.ShapeDtypeStruct((rows, D_MODEL), F32),
                   jax.ShapeDtypeStruct((nb, tail_total, D_MODEL), F32)],
        scratch_shapes=[pltpu.VMEM((hist + tm, D_MODEL), F32)],
        compiler_params=_cparams(2, vm),
        name="conv_layer",
    )(x, nw, w_in, cw, w_out, halo)


def _ab_in_kernel(x_ref, nw_ref, w_ref, cos_ref, sin_ref, pw_ref, ps_ref, halo_ref,
                  q0_ref, q1_ref, q2_ref, p_ref, win0_ref, win1_ref, win2_ref, tail_ref, ext, perm,
                  *, tm, stride, dils, pos0, hist, tail_rows, win_first, win_chunks):
    qkv_refs = (q0_ref, q1_ref, q2_ref)
    win_refs = (win0_ref, win1_ref, win2_ref)
    t = pl.program_id(1)
    h = _rms(x_ref[...], nw_ref[...]).astype(BF16)

    cos = jnp.concatenate([cos_ref[...]] * (A_WIDTH // V7X_LANES), axis=1)
    sin = jnp.concatenate([sin_ref[...]] * (A_WIDTH // V7X_LANES), axis=1)
    lane = lax.broadcasted_iota(jnp.int32, (tm, A_WIDTH), 1)
    first_half = (lane % HEAD_DIM) < (HEAD_DIM // 2)

    def rope(z):
        partner = jnp.where(first_half, pltpu.roll(z, A_WIDTH - HEAD_DIM // 2, 1), pltpu.roll(z, HEAD_DIM // 2, 1))
        return z * cos + partner * sin

    for g in range(N_DIL):
        base, d = g * GROUP_W, dils[g]

        def emit(part, z, g=g, d=d):
            out = qkv_refs[g]
            if d == 1:
                out[:, part * A_WIDTH:(part + 1) * A_WIDTH] = z.astype(BF16)
                return
            for c in range(LANE_CHUNKS):
                perm[c] = z[:, c * V7X_LANES:(c + 1) * V7X_LANES]
            for r in range(d):
                lo = r * GROUP_W + part * A_WIDTH
                rows_r = [perm[c, pl.ds(r, tm // d, stride=d), :] for c in range(LANE_CHUNKS)]
                out[:, lo:lo + A_WIDTH] = jnp.concatenate(rows_r, axis=1).astype(BF16)

        emit(0, rope(_dot(h, w_ref[:, base:base + A_WIDTH])) * (HEAD_DIM ** -0.5))
        k = rope(_dot(h, w_ref[:, base + A_WIDTH:base + 2 * A_WIDTH]))
        emit(1, k)
        v = _dot(h, w_ref[:, base + 2 * A_WIDTH:base + 3 * A_WIDTH])
        emit(2, v)

        def write_rows(g=g, k=k, v=v):
            for dst, lo, width in win_chunks[g]:
                win_refs[g][dst, 0:A_WIDTH, :] = k[lo:lo + width].T
                win_refs[g][dst, A_WIDTH:KV_W, :] = v[lo:lo + width].T

        if win_first[g] == 0:
            write_rows()
        else:
            pl.when(t >= win_first[g])(write_rows)

    _load_history(ext, halo_ref, _dot(h, w_ref[:, QKV_W:QKV_W + D_POOL]), hist, tm)
    if pos0 + 1 >= max(POOL_WINDOWS):
        pos = None
    else:
        pos = pos0 + t * (tm // stride) + lax.broadcasted_iota(jnp.int32, (tm, 1), 0) // stride
    for gi, w in enumerate(POOL_WINDOWS):
        sl = slice(gi * POOL_GC, (gi + 1) * POOL_GC)
        tok = ext[hist:hist + tm, sl]
        acc = tok
        for back in range(1, w):
            lo = hist - back * stride
            acc = acc + ext[lo:lo + tm, sl]
        cnt = float(w) if pos is None else jnp.minimum(w, pos + 1).astype(F32)
        d_tok = acc / cnt - tok
        p_ref[:, sl] = (_dot(d_tok.astype(BF16), pw_ref[gi]) * ps_ref[:, sl]).astype(BF16)
    tail_ref[...] = ext[hist + tm - tail_rows:hist + tm, :]
    _carry_history(ext, hist, tm)


def _ab_in(x, nw, w_in, cos, sin, pool_w, pool_scale, layer, halo, *, nb, nt, tm, stride, dils, pos0, tail_rows,
           tail_per_tile, win_shapes, win_blocks, win_index, win_first, win_chunks):
    rows = x.shape[0]
    hist = _hist_rows(POOL_STATE, stride)
    assert halo.shape == (nb, hist, D_POOL)
    assert cos.shape == (nt * tm, V7X_LANES)
    table_index = lambda b, t: (t, 0)
    seq_rows = nt * tm
    tail_total, tail_spec = _tail_spec(tail_rows, D_POOL, nt, tail_per_tile)
    qkv_shapes = [jax.ShapeDtypeStruct((nb, seq_rows // d, d * GROUP_W), BF16) for d in dils]
    qkv_specs = [pl.BlockSpec((None, tm // d, d * GROUP_W), lambda b, t: (b, t, 0)) for d in dils]
    vm = (2 * _nbytes((tm, D_MODEL), F32) + _nbytes((D_MODEL, QKV_W + D_POOL), BF16) + 2 * _nbytes((tm, QKV_W), BF16)
          + 2 * sum(_nbytes(blk, F32) for blk in win_blocks) + 4 * _nbytes((tm, D_POOL), F32)
          + _nbytes((2 * hist + tm, D_POOL), F32) + 2 * _nbytes((tail_rows, D_POOL), F32) + 8 * _nbytes((tm, A_WIDTH), F32))
    return pl.pallas_call(
        functools.partial(_ab_in_kernel, tm=tm, stride=stride, dils=dils, pos0=pos0, hist=hist, tail_rows=tail_rows,
                          win_first=win_first, win_chunks=win_chunks),
        grid=(nb, nt),
        in_specs=[pl.BlockSpec((tm, D_MODEL), lambda b, t: (b * nt + t, 0)),
                  _layer_block((1, D_MODEL), layer), _layer_block((D_MODEL, QKV_W + D_POOL), layer),
                  pl.BlockSpec((tm, V7X_LANES), table_index), pl.BlockSpec((tm, V7X_LANES), table_index),
                  _layer_block((len(POOL_WINDOWS), POOL_GC, POOL_GC), layer), _layer_block((1, D_POOL), layer),
                  pl.BlockSpec((None, hist, D_POOL), lambda b, t: (b, 0, 0))],
        out_specs=qkv_specs + [pl.BlockSpec((tm, D_POOL), lambda b, t: (b * nt + t, 0))]
                  + [pl.BlockSpec(blk, idx) for blk, idx in zip(win_blocks, win_index)] + [tail_spec],
        out_shape=qkv_shapes + [jax.ShapeDtypeStruct((rows, D_POOL), BF16)]
                  + [jax.ShapeDtypeStruct(s, F32) for s in win_shapes]
                  + [jax.ShapeDtypeStruct((nb, tail_total, D_POOL), F32)],
        scratch_shapes=[pltpu.VMEM((hist + tm, D_POOL), F32), pltpu.VMEM((LANE_CHUNKS, tm, V7X_LANES), F32)],
        compiler_params=_cparams(2, vm),
        name="ab_in",
    )(x, nw, w_in, cos, sin, pool_w, pool_scale, halo)


def _attn_kernel(*refs, merge, other_dils):
    if merge:
        q_ref, kc_ref, kp_ref, vc_ref, vp_ref, oa_ref, la_ref, ob_ref, lb_ref, out_ref, oa_n, la_n, ob_n, lb_n = refs
    else:
        q_ref, kc_ref, kp_ref, vc_ref, vp_ref, out_ref, lse_ref = refs
    n = pl.program_id(2)
    blk = ATT_BLK
    qi = lax.broadcasted_iota(jnp.int32, (blk, 2 * blk), 0)
    kj = lax.broadcasted_iota(jnp.int32, (blk, 2 * blk), 1)
    band = (kj >= qi) & (kj <= qi + SPAN)
    band_first = band & ((kj >= blk) | (n > 0))
    lane = lax.broadcasted_iota(jnp.int32, (blk, V7X_LANES), 1)
    even_head = lane < HEAD_DIM
    lane_head = lane // LSE_LANES

    if merge:
        for (o_ref, l_ref, o_n, l_n), d in zip(((oa_ref, la_ref, oa_n, la_n), (ob_ref, lb_ref, ob_n, lb_n)), other_dils):
            for r in range(d):
                dst = pl.ds(r, ATT_TQ // d, stride=d)
                for hp in range(LANE_CHUNKS):
                    lo = r * A_WIDTH + hp * V7X_LANES
                    o_n[hp, dst, :] = o_ref[:, lo:lo + V7X_LANES].astype(F32)
                l_n[dst, :] = l_ref[:, r * V7X_LANES:(r + 1) * V7X_LANES]

    for c in range(ATT_TQ // blk):
        rows = slice(c * blk, (c + 1) * blk)
        mask = band_first if c == 0 else band
        mask2 = jnp.concatenate([mask, mask], axis=0)
        lse_blk = jnp.zeros((blk, V7X_LANES), F32)
        outs = []
        for hp in range(A_HEADS // 2):
            ls = slice(hp * V7X_LANES, (hp + 1) * V7X_LANES)
            q2 = q_ref[rows, ls].astype(F32)
            qs = jnp.concatenate([jnp.where(even_head, q2, 0.0), jnp.where(even_head, 0.0, q2)], axis=0).astype(BF16)
            if c == 0:
                k2 = jnp.concatenate([kp_ref[:, ls], kc_ref[0:blk, ls]], axis=0)
                v2 = jnp.concatenate([vp_ref[:, ls], vc_ref[0:blk, ls]], axis=0)
            else:
                k2 = kc_ref[(c - 1) * blk:(c + 1) * blk, ls]
                v2 = vc_ref[(c - 1) * blk:(c + 1) * blk, ls]
            s = jnp.where(mask2, _dot_nt(qs, k2), NEG_INF)
            m = jnp.max(s, axis=-1, keepdims=True)
            p = jnp.exp(s - m)
            den = jnp.sum(p, axis=-1, keepdims=True)
            o = _dot(p.astype(BF16), v2) / den
            lse = m + jnp.log(den)
            outs.append(jnp.where(even_head, o[0:blk], o[blk:2 * blk]))
            lse_blk = jnp.where(lane_head == 2 * hp, lse[0:blk],
                                jnp.where(lane_head == 2 * hp + 1, lse[blk:2 * blk], lse_blk))
        if not merge:
            for hp, o2 in enumerate(outs):
                out_ref[rows, hp * V7X_LANES:(hp + 1) * V7X_LANES] = o2.astype(BF16)
            lse_ref[rows, :] = lse_blk
            continue
        la, lb = la_n[rows, :], lb_n[rows, :]
        mx = jnp.maximum(lse_blk, jnp.maximum(la, lb))
        e1, ea, eb = jnp.exp(lse_blk - mx), jnp.exp(la - mx), jnp.exp(lb - mx)
        tot = e1 + ea + eb
        w1, wa, wb = e1 / tot, ea / tot, eb / tot
        for hp, o2 in enumerate(outs):
            ls = slice(hp * V7X_LANES, (hp + 1) * V7X_LANES)
            ce, co = 2 * hp * LSE_LANES, (2 * hp + 1) * LSE_LANES
            spread = lambda w: jnp.where(even_head, w[:, ce:ce + 1], w[:, co:co + 1])
            a2 = spread(w1) * o2 + spread(wa) * oa_n[hp, rows, :] + spread(wb) * ob_n[hp, rows, :]
            out_ref[rows, ls] = a2.astype(BF16)


def _attn_group(qkv_g, g, nb, seq, others=None):
    dil = DIL_PATTERNS[g][1]
    sub = seq // dil
    tq, blk = ATT_TQ, ATT_BLK
    merge = others is not None
    parts = GROUP_W // A_WIDTH
    cur = lambda part: pl.BlockSpec((None, tq, A_WIDTH), lambda b, r, n: (b, n, r * parts + part))
    prev = lambda part: pl.BlockSpec((None, blk, A_WIDTH),
                                     lambda b, r, n: (b, jnp.maximum(n * (tq // blk) - 1, 0), r * parts + part))
    in_specs = [cur(0), cur(1), prev(1), cur(2), prev(2)]
    args = [qkv_g] * 5
    o_spec = pl.BlockSpec((None, tq, A_WIDTH), lambda b, r, n: (b, n, r))
    l_spec = pl.BlockSpec((None, tq, V7X_LANES), lambda b, r, n: (b, n, r))
    o_shape = jax.ShapeDtypeStruct((nb, sub, dil * A_WIDTH), BF16)
    scratch = []
    other_dils = ()
    if merge:
        assert dil == 1
        for o, lse, d in others:
            in_specs += [pl.BlockSpec((None, tq // d, d * A_WIDTH), lambda b, r, n: (b, n, 0)),
                         pl.BlockSpec((None, tq // d, d * V7X_LANES), lambda b, r, n: (b, n, 0))]
            args += [o, lse]
            scratch += [pltpu.VMEM((LANE_CHUNKS, tq, V7X_LANES), F32), pltpu.VMEM((tq, V7X_LANES), F32)]
        other_dils = tuple(d for _, _, d in others)
        out_specs, out_shape = o_spec, o_shape
    else:
        out_specs = [o_spec, l_spec]
        out_shape = [o_shape, jax.ShapeDtypeStruct((nb, sub, dil * V7X_LANES), F32)]
    vm = (2 * (10 * _nbytes((tq, A_WIDTH), BF16) + 3 * _nbytes((tq, V7X_LANES), F32))
          + 2 * _nbytes((tq, A_WIDTH + V7X_LANES), F32))
    res = pl.pallas_call(
        functools.partial(_attn_kernel, merge=merge, other_dils=other_dils),
        grid=(nb, dil, sub // tq),
        in_specs=in_specs,
        out_specs=out_specs,
        out_shape=out_shape,
        scratch_shapes=scratch,
        compiler_params=_cparams(3, vm),
        name=f"attn_d{dil}",
    )(*args)
    if merge:
        return res.reshape(nb * seq, A_WIDTH)
    return res[0], res[1]


def _attn_s_kernel(qkv_ref, c0_ref, c1_ref, c2_ref, out_ref, *, seg):
    hq = A_HEADS * seg
    row = lax.broadcasted_iota(jnp.int32, (hq, A_WIDTH), 0)
    lane = lax.broadcasted_iota(jnp.int32, (hq, A_WIDTH), 1)
    head_lanes = (lane // HEAD_DIM) == (row // seg)
    out_lane = lax.broadcasted_iota(jnp.int32, (seg, A_WIDTH), 1) // HEAD_DIM
    pad = V7X_LANES - seg
    new = qkv_ref[...]
    scores, values = [], []
    for g, ((win, dil), c_ref) in enumerate(zip(DIL_PATTERNS, (c0_ref, c1_ref, c2_ref))):
        base = g * GROUP_W
        q = new[:, base:base + A_WIDTH]
        qrows = jnp.where(head_lanes, jnp.concatenate([q] * A_HEADS, axis=0), 0.0).astype(BF16)
        nk = c_ref.shape[1]
        tok = lax.broadcasted_iota(jnp.int32, (hq, nk), 0) % seg
        key = lax.broadcasted_iota(jnp.int32, (hq, nk), 1)
        valid = ((key % dil) == (tok % dil)) & (key >= tok)
        scores.append(jnp.where(valid, _dot(qrows, c_ref[0:A_WIDTH, :].astype(BF16)), NEG_INF))
        values.append((c_ref[A_WIDTH:KV_W, :].astype(BF16), True))
        kn = jnp.concatenate([new[:, base + A_WIDTH:base + 2 * A_WIDTH], jnp.zeros((pad, A_WIDTH), F32)], axis=0)
        vn = jnp.concatenate([new[:, base + 2 * A_WIDTH:base + 3 * A_WIDTH], jnp.zeros((pad, A_WIDTH), F32)], axis=0)
        tok_n = lax.broadcasted_iota(jnp.int32, (hq, V7X_LANES), 0) % seg
        key_n = lax.broadcasted_iota(jnp.int32, (hq, V7X_LANES), 1)
        valid_n = (key_n <= tok_n) & (((tok_n - key_n) % dil) == 0)
        scores.append(jnp.where(valid_n, _dot_nt(qrows, kn.astype(BF16)), NEG_INF))
        values.append((vn.astype(BF16), False))
    m = scores[0].max(axis=-1, keepdims=True)
    for sc in scores[1:]:
        m = jnp.maximum(m, sc.max(axis=-1, keepdims=True))
    den = jnp.zeros((hq, 1), F32)
    acc = jnp.zeros((hq, A_WIDTH), F32)
    for sc, (val, transposed) in zip(scores, values):
        p = jnp.exp(sc - m)
        den = den + p.sum(axis=-1, keepdims=True)
        acc = acc + (_dot_nt(p.astype(BF16), val) if transposed else _dot(p.astype(BF16), val))
    acc = acc / den
    o = jnp.zeros((seg, A_WIDTH), F32)
    for hh in range(A_HEADS):
        o = jnp.where(out_lane == hh, acc[hh * seg:(hh + 1) * seg], o)
    out_ref[...] = o


def _attn_sample(qkv, caches_t, j, nseq, seg):
    assert all(c.shape[3] == dil * SPAN for c, (_, dil) in zip(caches_t, DIL_PATTERNS))
    cache_bytes = sum(_nbytes(c.shape[2:], F32) for c in caches_t)
    vm = 2 * (_nbytes((seg, QKV_W), F32) + cache_bytes) + cache_bytes // 2 + 8 * _nbytes((A_HEADS * seg, caches_t[2].shape[3]), F32)
    return pl.pallas_call(
        functools.partial(_attn_s_kernel, seg=seg),
        grid=(nseq,),
        in_specs=[pl.BlockSpec((seg, QKV_W), lambda i: (i, 0))]
                 + [pl.BlockSpec((None, None, KV_W, c.shape[3]), lambda i: (j, i, 0, 0)) for c in caches_t],
        out_specs=pl.BlockSpec((seg, A_WIDTH), lambda i: (i, 0)),
        out_shape=jax.ShapeDtypeStruct((nseq * seg, A_WIDTH), F32),
        compiler_params=_cparams(1, vm),
        name="attn_sample",
    )(qkv, *caches_t)


def _ab_out_kernel(x_ref, a_ref, p_ref, w_ref, o_ref):
    o_ref[...] = (x_ref[...] + _dot(a_ref[...], w_ref[0:A_WIDTH, :]) + _dot(p_ref[...], w_ref[A_WIDTH:A_WIDTH + D_POOL, :]))


def _ab_out(x, a, p, w_out, layer):
    rows = x.shape[0]
    tm = ROW_TILE
    vm = 4 * _nbytes((tm, D_MODEL), F32) + 4 * _nbytes((tm, A_WIDTH), BF16) + _nbytes((D_MODEL, D_MODEL), BF16)
    return pl.pallas_call(
        _ab_out_kernel,
        grid=(rows // tm,),
        in_specs=[pl.BlockSpec((tm, D_MODEL), lambda i: (i, 0)), pl.BlockSpec((tm, A_WIDTH), lambda i: (i, 0)),
                  pl.BlockSpec((tm, D_POOL), lambda i: (i, 0)), _layer_block((A_WIDTH + D_POOL, D_MODEL), layer)],
        out_specs=pl.BlockSpec((tm, D_MODEL), lambda i: (i, 0)),
        out_shape=jax.ShapeDtypeStruct((rows, D_MODEL), F32),
        compiler_params=_cparams(1, vm),
        name="ab_out",
    )(x, a, p, w_out)


def _rope_tables(pos):
    half = HEAD_DIM // 2
    inv = jnp.power(ROPE_THETA, -jnp.arange(half, dtype=F32) / half)
    reps = V7X_LANES // half
    ang = pos.astype(F32)[:, None] * jnp.tile(inv, reps)[None, :]
    sign = jnp.tile(jnp.concatenate([-jnp.ones((half,), F32), jnp.ones((half,), F32)]), reps // 2)
    return jnp.cos(ang), jnp.sin(ang) * sign[None, :]


def _prompt_trunk(x, w, nb, seq):
    tm = ROW_TILE
    nt = seq // tm
    depth = w["ffn1_norm"].shape[0]
    cos, sin = _rope_tables(jnp.arange(seq, dtype=jnp.int32))
    dils = tuple(d for _, d in DIL_PATTERNS)
    win_tot = tuple(min(win, seq) for win, _ in DIL_PATTERNS)
    win_r = tuple(min(wt, tm) for wt in win_tot)
    win_first = tuple(nt - max(wt // tm, 1) for wt in win_tot)
    win_cfg = dict(
        win_shapes=[(nb, KV_W, wt) for wt in win_tot],
        win_blocks=[(1, KV_W, r) for r in win_r],
        win_index=[(lambda b, t, first=first: (b, 0, jnp.maximum(t - first, 0))) for first in win_first],
        win_first=win_first,
        win_chunks=tuple(((0, tm - r, r),) for r in win_r))
    pool_halo = jnp.zeros((nb, _hist_rows(POOL_STATE, 1), D_POOL), F32)
    conv_halo = jnp.zeros((nb, _hist_rows(CONV_W - 1, 1), D_MODEL), F32)
    wins = [[] for _ in range(N_DIL)]
    pools, convs = [], []
    for l in range(depth):
        j = l // 2
        x = _ffn(x, w["ffn1_norm"], w["ffn1_w_gu"], w["ffn1_w_down"], l)
        if l % 2 == 0:
            q0, q1, q2, p, w0, w1, w2, ptail = _ab_in_call(
                x, w, j, cos, sin, pool_halo, nb=nb, nt=nt, tm=tm, stride=1, dils=dils, pos0=0,
                tail_rows=pool_halo.shape[1], tail_per_tile=False, **win_cfg)
            ob, lb = _attn_group(q2, 2, nb, seq)
            oa, la = _attn_group(q1, 1, nb, seq)
            a = _attn_group(q0, 0, nb, seq, others=((oa, la, dils[1]), (ob, lb, dils[2])))
            x = _ab_out(x, a, p, w["ab_w_out"], j)
            for g, wg in enumerate((w0, w1, w2)):
                wins[g].append(wg)
            pools.append(ptail[:, ptail.shape[1] - POOL_STATE:])
        else:
            x, ctail = _conv_layer(x, w["mix_norm_c"], w["conv_w_in"], w["conv_w"], w["conv_w_out"], j, conv_halo,
                                   nb=nb, nt=nt, tm=tm, stride=1, tail_rows=conv_halo.shape[1], tail_per_tile=False)
            convs.append(ctail[:, ctail.shape[1] - (CONV_W - 1):])
        x = _ffn(x, w["ffn2_norm"], w["ffn2_w_gu"], w["ffn2_w_down"], l,
                 final_w=w["final_norm"] if l == depth - 1 else None)
    win_out = []
    for g in range(N_DIL):
        wt = jnp.stack(wins[g], axis=0).reshape(len(wins[g]), nb, 2, A_HEADS, HEAD_DIM, win_tot[g])
        win_out.append(jnp.transpose(wt, (0, 1, 5, 2, 3, 4)))
    return x, win_out, jnp.stack(pools, axis=0), jnp.stack(convs, axis=0)


def _ab_in_call(x, w, j, cos, sin, halo, **kw):
    return _ab_in(x, w["mix_norm_ab"], w["ab_w_in"], cos, sin, w["pool_w"], w["pool_scale"], j, halo, **kw)


def _sample_trunk(x, w, nseq, seg, caches, pool_state, conv_state):
    tm = SAMPLE_ROW_TILE
    rows = seg * nseq
    nt = rows // tm
    tok_tile = tm // nseq
    depth = w["ffn1_norm"].shape[0]
    cos, sin = _rope_tables(PAST_LEN + jnp.arange(seg, dtype=jnp.int32))
    cos, sin = jnp.repeat(cos, nseq, axis=0), jnp.repeat(sin, nseq, axis=0)
    win_cfg = dict(
        win_shapes=[(seg, KV_W, nseq)] * N_DIL,
        win_blocks=[(tok_tile, KV_W, nseq)] * N_DIL,
        win_index=[lambda b, t: (t, 0, 0)] * N_DIL,
        win_first=(0,) * N_DIL,
        win_chunks=(tuple((i, i * nseq, nseq) for i in range(tok_tile)),) * N_DIL)
    caches_t = [jnp.transpose(c, (0, 1, 3, 4, 5, 2)).reshape(c.shape[0], nseq, KV_W, c.shape[2]) for c in caches]
    wins = [[] for _ in range(N_DIL)]
    pools, convs = [], []
    for l in range(depth):
        j = l // 2
        x = _ffn(x, w["ffn1_norm"], w["ffn1_w_gu"], w["ffn1_w_down"], l)
        if l % 2 == 0:
            state = jnp.transpose(pool_state[j], (1, 0, 2))
            halo = state.reshape(1, POOL_STATE * nseq, D_POOL)
            q0, q1, q2, p, w0, w1, w2, u = _ab_in_call(
                x, w, j, cos, sin, halo, nb=1, nt=nt, tm=tm, stride=nseq, dils=(1,) * N_DIL, pos0=PAST_LEN,
                tail_rows=tm, tail_per_tile=True, **win_cfg)
            qkv = jnp.concatenate([q.reshape(seg, nseq, GROUP_W) for q in (q0, q1, q2)], axis=-1)
            qkv = jnp.transpose(qkv, (1, 0, 2)).reshape(rows, QKV_W).astype(F32)
            a = _attn_sample(qkv, caches_t, j, nseq, seg)
            a = jnp.transpose(a.reshape(nseq, seg, A_WIDTH), (1, 0, 2)).reshape(rows, A_WIDTH).astype(BF16)
            x = _ab_out(x, a, p, w["ab_w_out"], j)
            for g, wg in enumerate((w0, w1, w2)):
                wins[g].append(wg)
            u = u.reshape(seg, nseq, D_POOL)
            pools.append(jnp.transpose(jnp.concatenate([state, u], axis=0)[-POOL_STATE:], (1, 0, 2)))
        else:
            state = jnp.transpose(conv_state[j], (1, 0, 2))
            halo = state.reshape(1, (CONV_W - 1) * nseq, D_MODEL)
            x, cu = _conv_layer(x, w["mix_norm_c"], w["conv_w_in"], w["conv_w"], w["conv_w_out"], j, halo,
                                nb=1, nt=nt, tm=tm, stride=nseq, tail_rows=tm, tail_per_tile=True)
            cu = cu.reshape(seg, nseq, D_MODEL)
            convs.append(jnp.transpose(jnp.concatenate([state, cu], axis=0)[-(CONV_W - 1):], (1, 0, 2)))
        x = _ffn(x, w["ffn2_norm"], w["ffn2_w_gu"], w["ffn2_w_down"], l,
                 final_w=w["final_norm"] if l == depth - 1 else None)
    win_out = []
    for g in range(N_DIL):
        wt = jnp.stack(wins[g], axis=0).reshape(len(wins[g]), seg, 2, A_HEADS, HEAD_DIM, nseq)
        win_out.append(jnp.transpose(wt, (0, 5, 1, 2, 3, 4)))
    return x, win_out, jnp.stack(pools, axis=0), jnp.stack(convs, axis=0)


def kernel(x_prompt, x_sample, cache_win0, cache_win1, cache_win2, state_pool, state_conv, ffn1_norm, ffn1_w_gu, ffn1_w_down, mix_norm, ffn2_norm, ffn2_w_gu, ffn2_w_down, ab_w_in, ab_w_out, pool_w, pool_scale, conv_w_in, conv_w, conv_w_out, final_norm):
    batch, seq, _ = x_prompt.shape
    dbatch, dseq, _ = x_sample.shape
    assert seq % ROW_TILE == 0 and (dbatch * dseq) % ROW_TILE == 0
    assert SAMPLE_ROW_TILE % dbatch == 0 and (dbatch * dseq) % SAMPLE_ROW_TILE == 0 and dbatch % V7X_SUBLANES == 0
    assert all((seq // dil) % ATT_TQ == 0 and ROW_TILE % dil == 0 for _, dil in DIL_PATTERNS)
    assert dseq == V7X_SUBLANES and PAST_LEN + 1 >= max(POOL_WINDOWS) and PAST_LEN >= POOL_STATE
    assert all(c.shape[2] == win and win == dil * SPAN for c, (win, dil) in zip((cache_win0, cache_win1, cache_win2), DIL_PATTERNS))
    row3 = lambda a: a.reshape(a.shape[0], 1, a.shape[1])
    w = dict(
        ffn1_norm=row3(ffn1_norm), ffn1_w_gu=ffn1_w_gu.astype(BF16), ffn1_w_down=ffn1_w_down.astype(BF16),
        ffn2_norm=row3(ffn2_norm), ffn2_w_gu=ffn2_w_gu.astype(BF16), ffn2_w_down=ffn2_w_down.astype(BF16),
        mix_norm_ab=row3(mix_norm[0::2]), mix_norm_c=row3(mix_norm[1::2]),
        ab_w_in=ab_w_in.astype(BF16), ab_w_out=ab_w_out.astype(BF16), pool_w=pool_w.astype(BF16), pool_scale=row3(pool_scale),
        conv_w_in=conv_w_in.astype(BF16), conv_w=conv_w, conv_w_out=conv_w_out.astype(BF16), final_norm=final_norm)
    y_p, win_p, pool_p, conv_p = _prompt_trunk(x_prompt.reshape(batch * seq, D_MODEL), w, batch, seq)
    xs = jnp.transpose(x_sample, (1, 0, 2)).reshape(dseq * dbatch, D_MODEL)
    y_s, win_s, pool_s, conv_s = _sample_trunk(xs, w, dbatch, dseq, (cache_win0, cache_win1, cache_win2),
                                               state_pool, state_conv)
    y_s = jnp.transpose(y_s.reshape(dseq, dbatch, D_MODEL), (1, 0, 2))
    return (y_p.reshape(batch, seq, D_MODEL), y_s,
            win_p[0], win_p[1], win_p[2], pool_p, conv_p,
            win_s[0], win_s[1], win_s[2], pool_s, conv_s)
```

```python
import functools

import jax
import jax.numpy as jnp
from jax import lax
from jax.experimental import pallas as pl
from jax.experimental.pallas import tpu as pltpu

F32 = jnp.float32
BF16 = jnp.bfloat16

D_MODEL = 1024
HEAD_DIM = 64
A_HEADS = 8
A_WIDTH = A_HEADS * HEAD_DIM
DIL_PATTERNS = ((128, 1), (512, 4), (2048, 16))
N_DIL = len(DIL_PATTERNS)
SPAN = 128
GROUP_W = 3 * A_WIDTH
QKV_W = N_DIL * GROUP_W
KV_W = 2 * A_WIDTH
POOL_WINDOWS = (2, 4, 8, 16)
D_POOL = 512
POOL_GC = 128
POOL_STATE = 15
CONV_W = 3
D_FF = 2816
ROPE_THETA = 10000.0
EPS = 1e-6
PAST_LEN = 2048

V7X_LANES = 128
V7X_SUBLANES = 8
V7X_MXU_DIM = 256
V7X_VMEM_USABLE = 56 * 1024 * 1024
MIB = 1024 * 1024

ROW_TILE = 512
SAMPLE_ROW_TILE = 256
ATT_S_CHUNK = 512
ATT_TQ = 256
ATT_BLK = SPAN
LSE_LANES = V7X_LANES // A_HEADS
LANE_CHUNKS = A_WIDTH // V7X_LANES
NEG_INF = float("-inf")


def _round_up(n, m):
    return -(-n // m) * m


def _nbytes(shape, dtype):
    n = 1
    for s in shape:
        n *= s
    return n * jnp.dtype(dtype).itemsize


def _cparams(n_axes, block_bytes):
    limit = int(min(V7X_VMEM_USABLE, block_bytes * 1.25 + 12 * MIB))
    return pltpu.CompilerParams(dimension_semantics=("arbitrary",) * n_axes, vmem_limit_bytes=limit)


def _layer_block(shape, layer):
    nd = len(shape)
    return pl.BlockSpec((None,) + tuple(shape), lambda *_: (layer,) + (0,) * nd, pipeline_mode=pl.Buffered(1))


def _const_block(shape):
    nd = len(shape)
    return pl.BlockSpec(tuple(shape), lambda *_: (0,) * nd, pipeline_mode=pl.Buffered(1))


def _rms(x, w):
    return x * lax.rsqrt(jnp.mean(x * x, axis=-1, keepdims=True) + EPS) * w


def _dot(a, b):
    return jnp.dot(a, b, preferred_element_type=F32)


def _dot_nt(a, b):
    return lax.dot_general(a, b, (((1,), (1,)), ((), ())), preferred_element_type=F32)


def _ffn_kernel(*refs, premix, final):
    refs = list(refs)
    x_ref, nw_ref, wgu_ref, wd_ref = refs[:4]
    del refs[:4]
    x = x_ref[...]
    if premix:
        a_ref, p_ref, wo_ref = refs[:3]
        del refs[:3]
        x = x + _dot(a_ref[...], wo_ref[0:A_WIDTH, :]) + _dot(p_ref[...], wo_ref[A_WIDTH:A_WIDTH + D_POOL, :])
    if final:
        fw_ref = refs.pop(0)
    o_ref, a_scr = refs
    h = _rms(x, nw_ref[...]).astype(BF16)
    for lo in range(0, D_FF, V7X_MXU_DIM):
        g = _dot(h, wgu_ref[:, lo:lo + V7X_MXU_DIM])
        u = _dot(h, wgu_ref[:, D_FF + lo:D_FF + lo + V7X_MXU_DIM])
        a_scr[:, lo:lo + V7X_MXU_DIM] = (g * jax.nn.sigmoid(g) * u).astype(BF16)
    y = x + 0.5 * _dot(a_scr[...], wd_ref[...])
    if final:
        y = _rms(y, fw_ref[...])
    o_ref[...] = y


def _ffn(x, nw, wgu, wd, layer, *, final_w=None, premix=None):
    rows = x.shape[0]
    tm = ROW_TILE
    final = final_w is not None
    in_specs = [pl.BlockSpec((tm, D_MODEL), lambda i: (i, 0)),
                _layer_block((1, D_MODEL), layer), _layer_block((D_MODEL, 2 * D_FF), layer),
                _layer_block((D_FF, D_MODEL), layer)]
    args = [x, nw, wgu, wd]
    vm = (4 * _nbytes((tm, D_MODEL), F32) + _nbytes((D_MODEL, 2 * D_FF), BF16) + _nbytes((D_FF, D_MODEL), BF16)
          + _nbytes((tm, D_FF), BF16))
    if premix is not None:
        a, p, w_out, j = premix
        in_specs += [pl.BlockSpec((tm, A_WIDTH), lambda i: (i, 0)), pl.BlockSpec((tm, D_POOL), lambda i: (i, 0)),
                     _layer_block((A_WIDTH + D_POOL, D_MODEL), j)]
        args += [a, p, w_out]
        vm += 4 * _nbytes((tm, A_WIDTH), BF16) + _nbytes((D_MODEL, D_MODEL), BF16)
    if final:
        in_specs.append(_const_block((1, D_MODEL)))
        args.append(final_w.reshape(1, D_MODEL))
    return pl.pallas_call(
        functools.partial(_ffn_kernel, premix=premix is not None, final=final),
        grid=(rows // tm,),
        in_specs=in_specs,
        out_specs=pl.BlockSpec((tm, D_MODEL), lambda i: (i, 0)),
        out_shape=jax.ShapeDtypeStruct((rows, D_MODEL), F32),
        scratch_shapes=[pltpu.VMEM((tm, D_FF), BF16)],
        compiler_params=_cparams(1, vm),
        name="ffn" + ("_mix" if premix is not None else "") + ("_final" if final else ""),
    )(*args)


def _hist_rows(positions, stride):
    return _round_up(positions * stride, V7X_SUBLANES)


def _init_history(ext, halo_ref, hist):
    @pl.when(pl.program_id(1) == 0)
    def _():
        ext[0:hist, :] = halo_ref[...]


def _carry_history(ext, hist, tm):
    ext[0:hist, :] = ext[tm:tm + hist, :]


def _conv_kernel(x_ref, nw_ref, win_ref, cw_ref, wout_ref, halo_ref, o_ref, tail_ref, ext, *, tm, stride, hist, tail_rows):
    _init_history(ext, halo_ref, hist)
    x = x_ref[...]
    h = _rms(x, nw_ref[...]).astype(BF16)
    gb = _dot(h, win_ref[:, 0:D_MODEL])
    gc = _dot(h, win_ref[:, D_MODEL:2 * D_MODEL])
    v = _dot(h, win_ref[:, 2 * D_MODEL:3 * D_MODEL])
    ext[hist:hist + tm, :] = gc * v
    cw = cw_ref[...]
    y = cw[CONV_W - 1] * ext[hist:hist + tm, :]
    for back in range(1, CONV_W):
        lo = hist - back * stride
        y = y + cw[CONV_W - 1 - back] * ext[lo:lo + tm, :]
    o_ref[...] = x + _dot((gb * y).astype(BF16), wout_ref[...])
    tail_ref[...] = ext[hist + tm - tail_rows:hist + tm, :]
    _carry_history(ext, hist, tm)


def _tail_spec(tail_rows, width, nt, per_tile):
    if per_tile:
        return nt * tail_rows, pl.BlockSpec((None, tail_rows, width), lambda b, t: (b, t, 0))
    return tail_rows, pl.BlockSpec((None, tail_rows, width), lambda b, t: (b, 0, 0))


def _conv_layer(x, nw, w_in, cw, w_out, layer, halo, *, nb, nt, tm, stride, tail_rows, tail_per_tile):
    hist = _hist_rows(CONV_W - 1, stride)
    assert halo.shape == (nb, hist, D_MODEL)
    rows = x.shape[0]
    tail_total, tail_spec = _tail_spec(tail_rows, D_MODEL, nt, tail_per_tile)
    vm = (8 * _nbytes((tm, D_MODEL), F32) + _nbytes((D_MODEL, 4 * D_MODEL), BF16)
          + _nbytes((hist + tm, D_MODEL), F32) + 4 * _nbytes((hist + tail_rows, D_MODEL), F32))
    return pl.pallas_call(
        functools.partial(_conv_kernel, tm=tm, stride=stride, hist=hist, tail_rows=tail_rows),
        grid=(nb, nt),
        in_specs=[pl.BlockSpec((tm, D_MODEL), lambda b, t: (b * nt + t, 0)),
                  _layer_block((1, D_MODEL), layer), _layer_block((D_MODEL, 3 * D_MODEL), layer),
                  _layer_block((CONV_W, D_MODEL), layer), _layer_block((D_MODEL, D_MODEL), layer),
                  pl.BlockSpec((None, hist, D_MODEL), lambda b, t: (b, 0, 0))],
        out_specs=[pl.BlockSpec((tm, D_MODEL), lambda b, t: (b * nt + t, 0)), tail_spec],
        out_shape=[jax.ShapeDtypeStruct((rows, D_MODEL), F32),
                   jax.ShapeDtypeStruct((nb, tail_total, D_MODEL), F32)],
        scratch_shapes=[pltpu.VMEM((hist + tm, D_MODEL), F32)],
        compiler_params=_cparams(2, vm),
        name="conv_layer",
    )(x, nw, w_in, cw, w_out, halo)


def _ab_in_kernel(x_ref, nw_ref, w_ref, cos_ref, sin_ref, pw_ref, ps_ref, halo_ref,
                  q0_ref, q1_ref, q2_ref, p_ref, win0_ref, win1_ref, win2_ref, tail_ref, ext, perm,
                  *, tm, stride, dils, pos0, hist, tail_rows, win_first, win_chunks):
    qkv_refs = (q0_ref, q1_ref, q2_ref)
    win_refs = (win0_ref, win1_ref, win2_ref)
    t = pl.program_id(1)
    _init_history(ext, halo_ref, hist)
    h = _rms(x_ref[...], nw_ref[...]).astype(BF16)

    cos = jnp.concatenate([cos_ref[...]] * (A_WIDTH // V7X_LANES), axis=1)
    sin = jnp.concatenate([sin_ref[...]] * (A_WIDTH // V7X_LANES), axis=1)
    lane = lax.broadcasted_iota(jnp.int32, (tm, A_WIDTH), 1)
    first_half = (lane % HEAD_DIM) < (HEAD_DIM // 2)

    def rope(z):
        partner = jnp.where(first_half, pltpu.roll(z, A_WIDTH - HEAD_DIM // 2, 1), pltpu.roll(z, HEAD_DIM // 2, 1))
        return z * cos + partner * sin

    kv = []
    for g in range(N_DIL):
        base, d = g * GROUP_W, dils[g]

        def emit(part, z, g=g, d=d):
            out = qkv_refs[g]
            if d == 1:
                out[:, part * A_WIDTH:(part + 1) * A_WIDTH] = z.astype(BF16)
                return
            for c in range(LANE_CHUNKS):
                perm[c] = z[:, c * V7X_LANES:(c + 1) * V7X_LANES]
            for r in range(d):
                lo = r * GROUP_W + part * A_WIDTH
                rows_r = [perm[c, pl.ds(r, tm // d, stride=d), :] for c in range(LANE_CHUNKS)]
                out[:, lo:lo + A_WIDTH] = jnp.concatenate(rows_r, axis=1).astype(BF16)

        emit(0, rope(_dot(h, w_ref[:, base:base + A_WIDTH])) * (HEAD_DIM ** -0.5))
        k = rope(_dot(h, w_ref[:, base + A_WIDTH:base + 2 * A_WIDTH]))
        emit(1, k)
        v = _dot(h, w_ref[:, base + 2 * A_WIDTH:base + 3 * A_WIDTH])
        emit(2, v)
        kv.append((k, v))

    ext[hist:hist + tm, :] = _dot(h, w_ref[:, QKV_W:QKV_W + D_POOL])
    if pos0 + 1 >= max(POOL_WINDOWS):
        pos = None
    else:
        pos = pos0 + t * (tm // stride) + lax.broadcasted_iota(jnp.int32, (tm, 1), 0) // stride
    for gi, w in enumerate(POOL_WINDOWS):
        sl = slice(gi * POOL_GC, (gi + 1) * POOL_GC)
        tok = ext[hist:hist + tm, sl]
        acc = tok
        for back in range(1, w):
            lo = hist - back * stride
            acc = acc + ext[lo:lo + tm, sl]
        cnt = float(w) if pos is None else jnp.minimum(w, pos + 1).astype(F32)
        d_tok = acc / cnt - tok
        p_ref[:, sl] = (_dot(d_tok.astype(BF16), pw_ref[gi]) * ps_ref[:, sl]).astype(BF16)
    tail_ref[...] = ext[hist + tm - tail_rows:hist + tm, :]
    _carry_history(ext, hist, tm)

    for g, (k, v) in enumerate(kv):
        def write_rows(g=g, k=k, v=v):
            for dst, lo, width in win_chunks[g]:
                win_refs[g][dst, 0:A_WIDTH, :] = k[lo:lo + width].T
                win_refs[g][dst, A_WIDTH:KV_W, :] = v[lo:lo + width].T

        if win_first[g] == 0:
            write_rows()
        else:
            pl.when(t >= win_first[g])(write_rows)


def _ab_in(x, nw, w_in, cos, sin, pool_w, pool_scale, layer, halo, *, nb, nt, tm, stride, dils, pos0, tail_rows,
           tail_per_tile, win_shapes, win_blocks, win_index, win_first, win_chunks):
    rows = x.shape[0]
    hist = _hist_rows(POOL_STATE, stride)
    assert halo.shape == (nb, hist, D_POOL)
    assert cos.shape == (nt * tm, V7X_LANES)
    table_index = lambda b, t: (t, 0)
    seq_rows = nt * tm
    tail_total, tail_spec = _tail_spec(tail_rows, D_POOL, nt, tail_per_tile)
    qkv_shapes = [jax.ShapeDtypeStruct((nb, seq_rows // d, d * GROUP_W), BF16) for d in dils]
    qkv_specs = [pl.BlockSpec((None, tm // d, d * GROUP_W), lambda b, t: (b, t, 0)) for d in dils]
    vm = (2 * _nbytes((tm, D_MODEL), F32) + _nbytes((D_MODEL, QKV_W + D_POOL), BF16) + 2 * _nbytes((tm, QKV_W), BF16)
          + 2 * sum(_nbytes(blk, F32) for blk in win_blocks) + 4 * _nbytes((tm, D_POOL), F32)
          + _nbytes((2 * hist + tm, D_POOL), F32) + 2 * _nbytes((tail_rows, D_POOL), F32) + 8 * _nbytes((tm, A_WIDTH), F32))
    return pl.pallas_call(
        functools.partial(_ab_in_kernel, tm=tm, stride=stride, dils=dils, pos0=pos0, hist=hist, tail_rows=tail_rows,
                          win_first=win_first, win_chunks=win_chunks),
        grid=(nb, nt),
        in_specs=[pl.BlockSpec((tm, D_MODEL), lambda b, t: (b * nt + t, 0)),
                  _layer_block((1, D_MODEL), layer), _layer_block((D_MODEL, QKV_W + D_POOL), layer),
                  pl.BlockSpec((tm, V7X_LANES), table_index), pl.BlockSpec((tm, V7X_LANES), table_index),
                  _layer_block((len(POOL_WINDOWS), POOL_GC, POOL_GC), layer), _layer_block((1, D_POOL), layer),
                  pl.BlockSpec((None, hist, D_POOL), lambda b, t: (b, 0, 0))],
        out_specs=qkv_specs + [pl.BlockSpec((tm, D_POOL), lambda b, t: (b * nt + t, 0))]
                  + [pl.BlockSpec(blk, idx) for blk, idx in zip(win_blocks, win_index)] + [tail_spec],
        out_shape=qkv_shapes + [jax.ShapeDtypeStruct((rows, D_POOL), BF16)]
                  + [jax.ShapeDtypeStruct(s, F32) for s in win_shapes]
                  + [jax.ShapeDtypeStruct((nb, tail_total, D_POOL), F32)],
        scratch_shapes=[pltpu.VMEM((hist + tm, D_POOL), F32), pltpu.VMEM((LANE_CHUNKS, tm, V7X_LANES), F32)],
        compiler_params=_cparams(2, vm),
        name="ab_in",
    )(x, nw, w_in, cos, sin, pool_w, pool_scale, halo)


def _attn_kernel(*refs, merge, other_dils, side_seg):
    if merge:
        q_ref, kc_ref, kp_ref, vc_ref, vp_ref, oa_ref, la_ref, ob_ref, lb_ref, out_ref, oa_n, la_n, ob_n, lb_n = refs
    elif side_seg:
        q_ref, kc_ref, kp_ref, vc_ref, vp_ref, sq_ref, c0_ref, c1_ref, c2_ref, out_ref, lse_ref, sout_ref = refs
        _attn_s_body(sq_ref, (c0_ref, c1_ref, c2_ref), sout_ref, side_seg)
    else:
        q_ref, kc_ref, kp_ref, vc_ref, vp_ref, out_ref, lse_ref = refs
    n = pl.program_id(2)
    blk = ATT_BLK
    qi = lax.broadcasted_iota(jnp.int32, (blk, 2 * blk), 0)
    kj = lax.broadcasted_iota(jnp.int32, (blk, 2 * blk), 1)
    band = (kj >= qi) & (kj <= qi + SPAN)
    band_first = band & ((kj >= blk) | (n > 0))
    lane = lax.broadcasted_iota(jnp.int32, (blk, V7X_LANES), 1)
    even_head = lane < HEAD_DIM
    lane_head = lane // LSE_LANES

    if merge:
        for (o_ref, l_ref, o_n, l_n), d in zip(((oa_ref, la_ref, oa_n, la_n), (ob_ref, lb_ref, ob_n, lb_n)), other_dils):
            for r in range(d):
                dst = pl.ds(r, ATT_TQ // d, stride=d)
                for hp in range(LANE_CHUNKS):
                    lo = r * A_WIDTH + hp * V7X_LANES
                    o_n[hp, dst, :] = o_ref[:, lo:lo + V7X_LANES].astype(F32)
                l_n[dst, :] = l_ref[:, r * V7X_LANES:(r + 1) * V7X_LANES]

    for c in range(ATT_TQ // blk):
        rows = slice(c * blk, (c + 1) * blk)
        mask = band_first if c == 0 else band
        mask2 = jnp.concatenate([mask, mask], axis=0)
        lse_blk = jnp.zeros((blk, V7X_LANES), F32)
        outs = []
        for hp in range(A_HEADS // 2):
            ls = slice(hp * V7X_LANES, (hp + 1) * V7X_LANES)
            q2 = q_ref[rows, ls].astype(F32)
            qs = jnp.concatenate([jnp.where(even_head, q2, 0.0), jnp.where(even_head, 0.0, q2)], axis=0).astype(BF16)
            if c == 0:
                k2 = jnp.concatenate([kp_ref[:, ls], kc_ref[0:blk, ls]], axis=0)
                v2 = jnp.concatenate([vp_ref[:, ls], vc_ref[0:blk, ls]], axis=0)
            else:
                k2 = kc_ref[(c - 1) * blk:(c + 1) * blk, ls]
                v2 = vc_ref[(c - 1) * blk:(c + 1) * blk, ls]
            s = jnp.where(mask2, _dot_nt(qs, k2), NEG_INF)
            m = jnp.max(s, axis=-1, keepdims=True)
            p = jnp.exp(s - m)
            den = jnp.sum(p, axis=-1, keepdims=True)
            o = _dot(p.astype(BF16), v2) / den
            lse = m + jnp.log(den)
            outs.append(jnp.where(even_head, o[0:blk], o[blk:2 * blk]))
            lse_blk = jnp.where(lane_head == 2 * hp, lse[0:blk],
                                jnp.where(lane_head == 2 * hp + 1, lse[blk:2 * blk], lse_blk))
        if not merge:
            for hp, o2 in enumerate(outs):
                out_ref[rows, hp * V7X_LANES:(hp + 1) * V7X_LANES] = o2.astype(BF16)
            lse_ref[rows, :] = lse_blk
            continue
        la, lb = la_n[rows, :], lb_n[rows, :]
        mx = jnp.maximum(lse_blk, jnp.maximum(la, lb))
        e1, ea, eb = jnp.exp(lse_blk - mx), jnp.exp(la - mx), jnp.exp(lb - mx)
        tot = e1 + ea + eb
        w1, wa, wb = e1 / tot, ea / tot, eb / tot
        for hp, o2 in enumerate(outs):
            ls = slice(hp * V7X_LANES, (hp + 1) * V7X_LANES)
            ce, co = 2 * hp * LSE_LANES, (2 * hp + 1) * LSE_LANES
            spread = lambda w: jnp.where(even_head, w[:, ce:ce + 1], w[:, co:co + 1])
            a2 = spread(w1) * o2 + spread(wa) * oa_n[hp, rows, :] + spread(wb) * ob_n[hp, rows, :]
            out_ref[rows, ls] = a2.astype(BF16)


def _attn_group(qkv_g, g, nb, seq, others=None, side=None):
    dil = DIL_PATTERNS[g][1]
    sub = seq // dil
    tq, blk = ATT_TQ, ATT_BLK
    merge = others is not None
    nq = sub // tq
    parts = GROUP_W // A_WIDTH
    cur = lambda part: pl.BlockSpec((None, tq, A_WIDTH), lambda b, r, n: (b, n, r * parts + part))
    prev = lambda part: pl.BlockSpec((None, blk, A_WIDTH),
                                     lambda b, r, n: (b, jnp.maximum(n * (tq // blk) - 1, 0), r * parts + part))
    in_specs = [cur(0), cur(1), prev(1), cur(2), prev(2)]
    args = [qkv_g] * 5
    o_spec = pl.BlockSpec((None, tq, A_WIDTH), lambda b, r, n: (b, n, r))
    l_spec = pl.BlockSpec((None, tq, V7X_LANES), lambda b, r, n: (b, n, r))
    o_shape = jax.ShapeDtypeStruct((nb, sub, dil * A_WIDTH), BF16)
    scratch = []
    other_dils = ()
    if merge:
        assert dil == 1
        for o, lse, d in others:
            in_specs += [pl.BlockSpec((None, tq // d, d * A_WIDTH), lambda b, r, n: (b, n, 0)),
                         pl.BlockSpec((None, tq // d, d * V7X_LANES), lambda b, r, n: (b, n, 0))]
            args += [o, lse]
            scratch += [pltpu.VMEM((LANE_CHUNKS, tq, V7X_LANES), F32), pltpu.VMEM((tq, V7X_LANES), F32)]
        other_dils = tuple(d for _, _, d in others)
        out_specs, out_shape = o_spec, o_shape
    else:
        out_specs = [o_spec, l_spec]
        out_shape = [o_shape, jax.ShapeDtypeStruct((nb, sub, dil * V7X_LANES), F32)]
    vm = (2 * (10 * _nbytes((tq, A_WIDTH), BF16) + 3 * _nbytes((tq, V7X_LANES), F32))
          + 2 * _nbytes((tq, A_WIDTH + V7X_LANES), F32))
    side_seg = 0
    if side is not None:
        assert not merge
        s_qkv, caches_t, jc, first_seq, n_seq, side_seg = side
        steps = nb * dil * nq
        n = n_seq // steps
        assert n * steps == n_seq and first_seq % n == 0 and first_seq + n_seq <= caches_t[0].shape[1]
        step = lambda b, r, i: first_seq // n + (b * dil + r) * nq + i
        in_specs += ([pl.BlockSpec((n * side_seg, QKV_W), lambda b, r, i: (step(b, r, i), 0))]
                     + [pl.BlockSpec((None, n, KV_W, c.shape[3]), lambda b, r, i: (jc, step(b, r, i), 0, 0))
                        for c in caches_t])
        args += [s_qkv, *caches_t]
        out_specs.append(pl.BlockSpec((n * side_seg, A_WIDTH), lambda b, r, i: ((b * dil + r) * nq + i, 0)))
        out_shape.append(jax.ShapeDtypeStruct((n_seq * side_seg, A_WIDTH), F32))
        cache_bytes = n * sum(_nbytes(c.shape[2:], F32) for c in caches_t)
        vm += (2 * (cache_bytes + _nbytes((n * side_seg, QKV_W), F32))
               + 16 * _nbytes((A_HEADS * side_seg, ATT_S_CHUNK), F32) + 4 * _nbytes((A_WIDTH, ATT_S_CHUNK), F32))
    res = pl.pallas_call(
        functools.partial(_attn_kernel, merge=merge, other_dils=other_dils, side_seg=side_seg),
        grid=(nb, dil, nq),
        in_specs=in_specs,
        out_specs=out_specs,
        out_shape=out_shape,
        scratch_shapes=scratch,
        compiler_params=_cparams(3, vm),
        name=f"attn_d{dil}",
    )(*args)
    if merge:
        return res.reshape(nb * seq, A_WIDTH)
    return tuple(res)


def _attn_s_body(qkv_ref, c_refs, out_ref, seg):
    hq = A_HEADS * seg
    row = lax.broadcasted_iota(jnp.int32, (hq, A_WIDTH), 0)
    lane = lax.broadcasted_iota(jnp.int32, (hq, A_WIDTH), 1)
    head_lanes = (lane // HEAD_DIM) == (row // seg)
    out_lane = lax.broadcasted_iota(jnp.int32, (seg, A_WIDTH), 1) // HEAD_DIM
    pad = V7X_LANES - seg
    for s in range(c_refs[0].shape[0]):
        new = qkv_ref[s * seg:(s + 1) * seg, :]
        scores, values = [], []
        for g, ((win, dil), c_ref) in enumerate(zip(DIL_PATTERNS, c_refs)):
            base = g * GROUP_W
            q = new[:, base:base + A_WIDTH]
            qrows = jnp.where(head_lanes, jnp.concatenate([q] * A_HEADS, axis=0), 0.0).astype(BF16)
            nk = c_ref.shape[2]
            ch = min(nk, ATT_S_CHUNK)
            tok = lax.broadcasted_iota(jnp.int32, (hq, ch), 0) % seg
            key = lax.broadcasted_iota(jnp.int32, (hq, ch), 1)
            same_class = (key % dil) == (tok % dil)
            for lo in range(0, nk, ch):
                valid = same_class & (key >= tok) if lo < seg else same_class
                kt = c_ref[s, 0:A_WIDTH, lo:lo + ch].astype(BF16)
                scores.append(jnp.where(valid, _dot(qrows, kt), NEG_INF))
                values.append((c_ref, lo, ch))
            kn = jnp.concatenate([new[:, base + A_WIDTH:base + 2 * A_WIDTH], jnp.zeros((pad, A_WIDTH), F32)], axis=0)
            vn = jnp.concatenate([new[:, base + 2 * A_WIDTH:base + 3 * A_WIDTH], jnp.zeros((pad, A_WIDTH), F32)], axis=0)
            tok_n = lax.broadcasted_iota(jnp.int32, (hq, V7X_LANES), 0) % seg
            key_n = lax.broadcasted_iota(jnp.int32, (hq, V7X_LANES), 1)
            valid_n = (key_n <= tok_n) & (((tok_n - key_n) % dil) == 0)
            scores.append(jnp.where(valid_n, _dot_nt(qrows, kn.astype(BF16)), NEG_INF))
            values.append(vn.astype(BF16))
        m = scores[0].max(axis=-1, keepdims=True)
        for sc in scores[1:]:
            m = jnp.maximum(m, sc.max(axis=-1, keepdims=True))
        den = jnp.zeros((hq, 1), F32)
        acc = jnp.zeros((hq, A_WIDTH), F32)
        for sc, val in zip(scores, values):
            p = jnp.exp(sc - m)
            den = den + p.sum(axis=-1, keepdims=True)
            if isinstance(val, tuple):
                c_ref, lo, ch = val
                acc = acc + _dot_nt(p.astype(BF16), c_ref[s, A_WIDTH:KV_W, lo:lo + ch].astype(BF16))
            else:
                acc = acc + _dot(p.astype(BF16), val)
        acc = acc / den
        o = jnp.zeros((seg, A_WIDTH), F32)
        for hh in range(A_HEADS):
            o = jnp.where(out_lane == hh, acc[hh * seg:(hh + 1) * seg], o)
        out_ref[s * seg:(s + 1) * seg, :] = o


def _rope_tables(pos):
    half = HEAD_DIM // 2
    inv = jnp.power(ROPE_THETA, -jnp.arange(half, dtype=F32) / half)
    reps = V7X_LANES // half
    ang = pos.astype(F32)[:, None] * jnp.tile(inv, reps)[None, :]
    sign = jnp.tile(jnp.concatenate([-jnp.ones((half,), F32), jnp.ones((half,), F32)]), reps // 2)
    return jnp.cos(ang), jnp.sin(ang) * sign[None, :]


def _prompt_mixers(w, nb, seq):
    tm = ROW_TILE
    nt = seq // tm
    cos, sin = _rope_tables(jnp.arange(seq, dtype=jnp.int32))
    dils = tuple(d for _, d in DIL_PATTERNS)
    win_tot = tuple(min(win, seq) for win, _ in DIL_PATTERNS)
    win_r = tuple(min(wt, tm) for wt in win_tot)
    win_first = tuple(nt - max(wt // tm, 1) for wt in win_tot)
    win_cfg = dict(
        win_shapes=[(nb, KV_W, wt) for wt in win_tot],
        win_blocks=[(1, KV_W, r) for r in win_r],
        win_index=[(lambda b, t, first=first: (b, 0, jnp.maximum(t - first, 0))) for first in win_first],
        win_first=win_first,
        win_chunks=tuple(((0, tm - r, r),) for r in win_r))
    pool_halo = jnp.zeros((nb, _hist_rows(POOL_STATE, 1), D_POOL), F32)
    conv_halo = jnp.zeros((nb, _hist_rows(CONV_W - 1, 1), D_MODEL), F32)

    def ab_mixer(x, j, sample):
        q0, q1, q2, p, w0, w1, w2, ptail = _ab_in_call(
            x, w, j, cos, sin, pool_halo, nb=nb, nt=nt, tm=tm, stride=1, dils=dils, pos0=0,
            tail_rows=pool_halo.shape[1], tail_per_tile=False, **win_cfg)
        s_qkv, caches_t, nseq, seg = sample
        half = nseq // 2
        ob, lb, att_lo = _attn_group(q2, 2, nb, seq, side=(s_qkv, caches_t, j, 0, half, seg))
        oa, la, att_hi = _attn_group(q1, 1, nb, seq, side=(s_qkv, caches_t, j, half, nseq - half, seg))
        a = _attn_group(q0, 0, nb, seq, others=((oa, la, dils[1]), (ob, lb, dils[2])))
        return a, p, (w0, w1, w2), ptail[:, ptail.shape[1] - POOL_STATE:], jnp.concatenate([att_lo, att_hi], axis=0)

    def conv_mixer(x, j):
        x, ctail = _conv_layer(x, w["mix_norm_c"], w["conv_w_in"], w["conv_w"], w["conv_w_out"], j, conv_halo,
                               nb=nb, nt=nt, tm=tm, stride=1, tail_rows=conv_halo.shape[1], tail_per_tile=False)
        return x, ctail[:, ctail.shape[1] - (CONV_W - 1):]

    def present_windows(wins):
        out = []
        for g in range(N_DIL):
            wt = jnp.stack(wins[g], axis=0).reshape(len(wins[g]), nb, 2, A_HEADS, HEAD_DIM, win_tot[g])
            out.append(jnp.transpose(wt, (0, 1, 5, 2, 3, 4)))
        return out

    return ab_mixer, conv_mixer, present_windows


def _ab_in_call(x, w, j, cos, sin, halo, **kw):
    return _ab_in(x, w["mix_norm_ab"], w["ab_w_in"], cos, sin, w["pool_w"], w["pool_scale"], j, halo, **kw)


def _sample_mixers(w, nseq, seg, pool_state, conv_state):
    tm = SAMPLE_ROW_TILE
    rows = seg * nseq
    nt = rows // tm
    tok_tile = tm // nseq
    cos, sin = _rope_tables(PAST_LEN + jnp.arange(seg, dtype=jnp.int32))
    cos, sin = jnp.repeat(cos, nseq, axis=0), jnp.repeat(sin, nseq, axis=0)
    win_cfg = dict(
        win_shapes=[(seg, KV_W, nseq)] * N_DIL,
        win_blocks=[(tok_tile, KV_W, nseq)] * N_DIL,
        win_index=[lambda b, t: (t, 0, 0)] * N_DIL,
        win_first=(0,) * N_DIL,
        win_chunks=(tuple((i, i * nseq, nseq) for i in range(tok_tile)),) * N_DIL)

    def ab_in(x, j):
        state = jnp.transpose(pool_state[j], (1, 0, 2))
        halo = state.reshape(1, POOL_STATE * nseq, D_POOL)
        q0, q1, q2, p, w0, w1, w2, u = _ab_in_call(
            x, w, j, cos, sin, halo, nb=1, nt=nt, tm=tm, stride=nseq, dils=(1,) * N_DIL, pos0=PAST_LEN,
            tail_rows=tm, tail_per_tile=True, **win_cfg)
        qkv = jnp.concatenate([q.reshape(seg, nseq, GROUP_W) for q in (q0, q1, q2)], axis=-1)
        qkv = jnp.transpose(qkv, (1, 0, 2)).reshape(rows, QKV_W).astype(F32)
        u = u.reshape(seg, nseq, D_POOL)
        pool = jnp.transpose(jnp.concatenate([state, u], axis=0)[-POOL_STATE:], (1, 0, 2))
        return qkv, p, (w0, w1, w2), pool

    def attn_rows(a):
        return jnp.transpose(a.reshape(nseq, seg, A_WIDTH), (1, 0, 2)).reshape(rows, A_WIDTH).astype(BF16)

    def conv_mixer(x, j):
        state = jnp.transpose(conv_state[j], (1, 0, 2))
        halo = state.reshape(1, (CONV_W - 1) * nseq, D_MODEL)
        x, cu = _conv_layer(x, w["mix_norm_c"], w["conv_w_in"], w["conv_w"], w["conv_w_out"], j, halo,
                            nb=1, nt=nt, tm=tm, stride=nseq, tail_rows=tm, tail_per_tile=True)
        cu = cu.reshape(seg, nseq, D_MODEL)
        return x, jnp.transpose(jnp.concatenate([state, cu], axis=0)[-(CONV_W - 1):], (1, 0, 2))

    def present_windows(wins):
        out = []
        for g in range(N_DIL):
            wt = jnp.stack(wins[g], axis=0).reshape(len(wins[g]), seg, 2, A_HEADS, HEAD_DIM, nseq)
            out.append(jnp.transpose(wt, (0, 5, 1, 2, 3, 4)))
        return out

    return ab_in, attn_rows, conv_mixer, present_windows


def _trunks(xp, xs, w, caches, pool_state, conv_state, *, nb, seq, nseq, seg):
    depth = w["ffn1_norm"].shape[0]
    p_ab, p_conv, p_windows = _prompt_mixers(w, nb, seq)
    s_ab_in, s_attn_rows, s_conv, s_windows = _sample_mixers(w, nseq, seg, pool_state, conv_state)
    caches_t = [jnp.transpose(c, (0, 1, 3, 4, 5, 2)).reshape(c.shape[0], nseq, KV_W, c.shape[2]) for c in caches]
    ffn1 = lambda x, l, **kw: _ffn(x, w["ffn1_norm"], w["ffn1_w_gu"], w["ffn1_w_down"], l, **kw)
    ffn2 = lambda x, l, **kw: _ffn(x, w["ffn2_norm"], w["ffn2_w_gu"], w["ffn2_w_down"], l,
                                   final_w=w["final_norm"] if l == depth - 1 else None, **kw)
    wins_p, wins_s = [[] for _ in range(N_DIL)], [[] for _ in range(N_DIL)]
    pools_p, pools_s, convs_p, convs_s = [], [], [], []
    for l in range(depth):
        j = l // 2
        if l % 2 == 0:
            xs = ffn1(xs, l)
            qkv_s, p_s, win_s, pool_s = s_ab_in(xs, j)
            xp = ffn1(xp, l)
            a_p, p_p, win_p, pool_p, att_s = p_ab(xp, j, (qkv_s, caches_t, nseq, seg))
            xp = ffn2(xp, l, premix=(a_p, p_p, w["ab_w_out"], j))
            xs = ffn2(xs, l, premix=(s_attn_rows(att_s), p_s, w["ab_w_out"], j))
            for g in range(N_DIL):
                wins_p[g].append(win_p[g])
                wins_s[g].append(win_s[g])
            pools_p.append(pool_p)
            pools_s.append(pool_s)
        else:
            xp, conv_p = p_conv(ffn1(xp, l), j)
            xs, conv_s = s_conv(ffn1(xs, l), j)
            xp, xs = ffn2(xp, l), ffn2(xs, l)
            convs_p.append(conv_p)
            convs_s.append(conv_s)
    return ((xp, p_windows(wins_p), jnp.stack(pools_p, axis=0), jnp.stack(convs_p, axis=0)),
            (xs, s_windows(wins_s), jnp.stack(pools_s, axis=0), jnp.stack(convs_s, axis=0)))


def kernel(x_prompt, x_sample, cache_win0, cache_win1, cache_win2, state_pool, state_conv, ffn1_norm, ffn1_w_gu, ffn1_w_down, mix_norm, ffn2_norm, ffn2_w_gu, ffn2_w_down, ab_w_in, ab_w_out, pool_w, pool_scale, conv_w_in, conv_w, conv_w_out, final_norm):
    batch, seq, _ = x_prompt.shape
    dbatch, dseq, _ = x_sample.shape
    assert seq % ROW_TILE == 0 and (dbatch * dseq) % ROW_TILE == 0 and dbatch % 2 == 0
    assert SAMPLE_ROW_TILE % dbatch == 0 and (dbatch * dseq) % SAMPLE_ROW_TILE == 0 and dbatch % V7X_SUBLANES == 0
    assert all((seq // dil) % ATT_TQ == 0 and ROW_TILE % dil == 0 for _, dil in DIL_PATTERNS)
    assert dseq == V7X_SUBLANES and PAST_LEN + 1 >= max(POOL_WINDOWS) and PAST_LEN >= POOL_STATE
    assert all(c.shape[2] == win and win == dil * SPAN for c, (win, dil) in zip((cache_win0, cache_win1, cache_win2), DIL_PATTERNS))
    row3 = lambda a: a.reshape(a.shape[0], 1, a.shape[1])
    w = dict(
        ffn1_norm=row3(ffn1_norm), ffn1_w_gu=ffn1_w_gu.astype(BF16), ffn1_w_down=ffn1_w_down.astype(BF16),
        ffn2_norm=row3(ffn2_norm), ffn2_w_gu=ffn2_w_gu.astype(BF16), ffn2_w_down=ffn2_w_down.astype(BF16),
        mix_norm_ab=row3(mix_norm[0::2]), mix_norm_c=row3(mix_norm[1::2]),
        ab_w_in=ab_w_in.astype(BF16), ab_w_out=ab_w_out.astype(BF16), pool_w=pool_w.astype(BF16), pool_scale=row3(pool_scale),
        conv_w_in=conv_w_in.astype(BF16), conv_w=conv_w, conv_w_out=conv_w_out.astype(BF16), final_norm=final_norm)
    xs = jnp.transpose(x_sample, (1, 0, 2)).reshape(dseq * dbatch, D_MODEL)
    (y_p, win_p, pool_p, conv_p), (y_s, win_s, pool_s, conv_s) = _trunks(
        x_prompt.reshape(batch * seq, D_MODEL), xs, w, (cache_win0, cache_win1, cache_win2), state_pool, state_conv,
        nb=batch, seq=seq, nseq=dbatch, seg=dseq)
    y_s = jnp.transpose(y_s.reshape(dseq, dbatch, D_MODEL), (1, 0, 2))
    return (y_p.reshape(batch, seq, D_MODEL), y_s,
            win_p[0], win_p[1], win_p[2], pool_p, conv_p,
            win_s[0], win_s[1], win_s[2], pool_s, conv_s)
```

```python
import functools

import jax
import jax.numpy as jnp
from jax import lax
from jax.experimental import pallas as pl
from jax.experimental.pallas import tpu as pltpu

F32 = jnp.float32
BF16 = jnp.bfloat16

D_MODEL = 1024
HEAD_DIM = 64
A_HEADS = 8
A_WIDTH = A_HEADS * HEAD_DIM
DIL_PATTERNS = ((128, 1), (512, 4), (2048, 16))
N_DIL = len(DIL_PATTERNS)
SPAN = 128
GROUP_W = 3 * A_WIDTH
QKV_W = N_DIL * GROUP_W
KV_W = 2 * A_WIDTH
POOL_WINDOWS = (2, 4, 8, 16)
D_POOL = 512
POOL_GC = 128
POOL_STATE = 15
CONV_W = 3
D_FF = 2816
ROPE_THETA = 10000.0
EPS = 1e-6
PAST_LEN = 2048

V7X_LANES = 128
V7X_SUBLANES = 8
V7X_MXU_DIM = 256
V7X_VMEM_USABLE = 56 * 1024 * 1024
MIB = 1024 * 1024

ROW_TILE = 512
MIXER_ROW_TILE = 512
SAMPLE_ROW_TILE = 256
ATT_S_CHUNK = 512
ATT_TQ = 256
ATT_BLK = SPAN
LSE_LANES = V7X_LANES // A_HEADS
LANE_CHUNKS = A_WIDTH // V7X_LANES
NEG_INF = float("-inf")


def _round_up(n, m):
    return -(-n // m) * m


def _nbytes(shape, dtype):
    n = 1
    for s in shape:
        n *= s
    return n * jnp.dtype(dtype).itemsize


def _cparams(n_axes, block_bytes):
    limit = int(min(V7X_VMEM_USABLE, block_bytes * 1.25 + 12 * MIB))
    return pltpu.CompilerParams(dimension_semantics=("arbitrary",) * n_axes, vmem_limit_bytes=limit)


def _layer_block(shape, layer):
    nd = len(shape)
    return pl.BlockSpec((None,) + tuple(shape), lambda *_: (layer,) + (0,) * nd, pipeline_mode=pl.Buffered(1))


def _const_block(shape):
    nd = len(shape)
    return pl.BlockSpec(tuple(shape), lambda *_: (0,) * nd, pipeline_mode=pl.Buffered(1))


def _rms(x, w):
    return x * lax.rsqrt(jnp.mean(x * x, axis=-1, keepdims=True) + EPS) * w


def _dot(a, b):
    return jnp.dot(a, b, preferred_element_type=F32)


def _dot_nt(a, b):
    return lax.dot_general(a, b, (((1,), (1,)), ((), ())), preferred_element_type=F32)


def _ffn_kernel(*refs, premix, final):
    refs = list(refs)
    x_ref, nw_ref, wgu_ref, wd_ref = refs[:4]
    del refs[:4]
    x = x_ref[...]
    if premix:
        a_ref, p_ref, wo_ref = refs[:3]
        del refs[:3]
        x = x + _dot(a_ref[...], wo_ref[0:A_WIDTH, :]) + _dot(p_ref[...], wo_ref[A_WIDTH:A_WIDTH + D_POOL, :])
    if final:
        fw_ref = refs.pop(0)
    o_ref, a_scr = refs
    h = _rms(x, nw_ref[...]).astype(BF16)
    for lo in range(0, D_FF, V7X_MXU_DIM):
        g = _dot(h, wgu_ref[:, lo:lo + V7X_MXU_DIM])
        u = _dot(h, wgu_ref[:, D_FF + lo:D_FF + lo + V7X_MXU_DIM])
        a_scr[:, lo:lo + V7X_MXU_DIM] = (g * jax.nn.sigmoid(g) * u).astype(BF16)
    y = x + 0.5 * _dot(a_scr[...], wd_ref[...])
    if final:
        y = _rms(y, fw_ref[...])
    o_ref[...] = y


def _ffn(x, nw, wgu, wd, layer, *, final_w=None, premix=None):
    rows = x.shape[0]
    tm = ROW_TILE
    final = final_w is not None
    in_specs = [pl.BlockSpec((tm, D_MODEL), lambda i: (i, 0)),
                _layer_block((1, D_MODEL), layer), _layer_block((D_MODEL, 2 * D_FF), layer),
                _layer_block((D_FF, D_MODEL), layer)]
    args = [x, nw, wgu, wd]
    vm = (4 * _nbytes((tm, D_MODEL), F32) + _nbytes((D_MODEL, 2 * D_FF), BF16) + _nbytes((D_FF, D_MODEL), BF16)
          + _nbytes((tm, D_FF), BF16))
    if premix is not None:
        a, p, w_out, j = premix
        in_specs += [pl.BlockSpec((tm, A_WIDTH), lambda i: (i, 0)), pl.BlockSpec((tm, D_POOL), lambda i: (i, 0)),
                     _layer_block((A_WIDTH + D_POOL, D_MODEL), j)]
        args += [a, p, w_out]
        vm += 4 * _nbytes((tm, A_WIDTH), BF16) + _nbytes((D_MODEL, D_MODEL), BF16)
    if final:
        in_specs.append(_const_block((1, D_MODEL)))
        args.append(final_w.reshape(1, D_MODEL))
    return pl.pallas_call(
        functools.partial(_ffn_kernel, premix=premix is not None, final=final),
        grid=(rows // tm,),
        in_specs=in_specs,
        out_specs=pl.BlockSpec((tm, D_MODEL), lambda i: (i, 0)),
        out_shape=jax.ShapeDtypeStruct((rows, D_MODEL), F32),
        scratch_shapes=[pltpu.VMEM((tm, D_FF), BF16)],
        compiler_params=_cparams(1, vm),
        name="ffn" + ("_mix" if premix is not None else "") + ("_final" if final else ""),
    )(*args)


def _hist_rows(positions, stride):
    return _round_up(positions * stride, V7X_SUBLANES)


def _init_history(ext, halo_ref, hist):
    @pl.when(pl.program_id(1) == 0)
    def _():
        ext[0:hist, :] = halo_ref[...]


def _carry_history(ext, hist, tm):
    ext[0:hist, :] = ext[tm:tm + hist, :]


def _conv_kernel(x_ref, nw_ref, win_ref, cw_ref, wout_ref, halo_ref, o_ref, tail_ref, ext, *, tm, stride, hist, tail_rows):
    _init_history(ext, halo_ref, hist)
    x = x_ref[...]
    h = _rms(x, nw_ref[...]).astype(BF16)
    gb = _dot(h, win_ref[:, 0:D_MODEL])
    gc = _dot(h, win_ref[:, D_MODEL:2 * D_MODEL])
    v = _dot(h, win_ref[:, 2 * D_MODEL:3 * D_MODEL])
    ext[hist:hist + tm, :] = gc * v
    cw = cw_ref[...]
    y = cw[CONV_W - 1] * ext[hist:hist + tm, :]
    for back in range(1, CONV_W):
        lo = hist - back * stride
        y = y + cw[CONV_W - 1 - back] * ext[lo:lo + tm, :]
    o_ref[...] = x + _dot((gb * y).astype(BF16), wout_ref[...])
    tail_ref[...] = ext[hist + tm - tail_rows:hist + tm, :]
    _carry_history(ext, hist, tm)


def _tail_spec(tail_rows, width, nt, per_tile):
    if per_tile:
        return nt * tail_rows, pl.BlockSpec((None, tail_rows, width), lambda b, t: (b, t, 0))
    return tail_rows, pl.BlockSpec((None, tail_rows, width), lambda b, t: (b, 0, 0))


def _conv_layer(x, nw, w_in, cw, w_out, layer, halo, *, nb, nt, tm, stride, tail_rows, tail_per_tile):
    hist = _hist_rows(CONV_W - 1, stride)
    assert halo.shape == (nb, hist, D_MODEL)
    rows = x.shape[0]
    tail_total, tail_spec = _tail_spec(tail_rows, D_MODEL, nt, tail_per_tile)
    vm = (8 * _nbytes((tm, D_MODEL), F32) + _nbytes((D_MODEL, 4 * D_MODEL), BF16)
          + _nbytes((hist + tm, D_MODEL), F32) + 4 * _nbytes((hist + tail_rows, D_MODEL), F32))
    return pl.pallas_call(
        functools.partial(_conv_kernel, tm=tm, stride=stride, hist=hist, tail_rows=tail_rows),
        grid=(nb, nt),
        in_specs=[pl.BlockSpec((tm, D_MODEL), lambda b, t: (b * nt + t, 0)),
                  _layer_block((1, D_MODEL), layer), _layer_block((D_MODEL, 3 * D_MODEL), layer),
                  _layer_block((CONV_W, D_MODEL), layer), _layer_block((D_MODEL, D_MODEL), layer),
                  pl.BlockSpec((None, hist, D_MODEL), lambda b, t: (b, 0, 0))],
        out_specs=[pl.BlockSpec((tm, D_MODEL), lambda b, t: (b * nt + t, 0)), tail_spec],
        out_shape=[jax.ShapeDtypeStruct((rows, D_MODEL), F32),
                   jax.ShapeDtypeStruct((nb, tail_total, D_MODEL), F32)],
        scratch_shapes=[pltpu.VMEM((hist + tm, D_MODEL), F32)],
        compiler_params=_cparams(2, vm),
        name="conv_layer",
    )(x, nw, w_in, cw, w_out, halo)


def _ab_in_kernel(x_ref, nw_ref, w_ref, cos_ref, sin_ref, pw_ref, ps_ref, halo_ref,
                  q0_ref, q1_ref, q2_ref, p_ref, win0_ref, win1_ref, win2_ref, tail_ref, ext, perm,
                  *, tm, stride, dils, pos0, hist, tail_rows, win_first, win_chunks):
    qkv_refs = (q0_ref, q1_ref, q2_ref)
    win_refs = (win0_ref, win1_ref, win2_ref)
    t = pl.program_id(1)
    _init_history(ext, halo_ref, hist)
    h = _rms(x_ref[...], nw_ref[...]).astype(BF16)

    cos = jnp.concatenate([cos_ref[...]] * (A_WIDTH // V7X_LANES), axis=1)
    sin = jnp.concatenate([sin_ref[...]] * (A_WIDTH // V7X_LANES), axis=1)
    lane = lax.broadcasted_iota(jnp.int32, (tm, A_WIDTH), 1)
    first_half = (lane % HEAD_DIM) < (HEAD_DIM // 2)

    def rope(z):
        partner = jnp.where(first_half, pltpu.roll(z, A_WIDTH - HEAD_DIM // 2, 1), pltpu.roll(z, HEAD_DIM // 2, 1))
        return z * cos + partner * sin

    kv = []
    for g in range(N_DIL):
        base, d = g * GROUP_W, dils[g]

        def emit(part, z, g=g, d=d):
            out = qkv_refs[g]
            if d == 1:
                out[:, part * A_WIDTH:(part + 1) * A_WIDTH] = z.astype(BF16)
                return
            for c in range(LANE_CHUNKS):
                perm[c] = z[:, c * V7X_LANES:(c + 1) * V7X_LANES]
            for r in range(d):
                lo = r * GROUP_W + part * A_WIDTH
                rows_r = [perm[c, pl.ds(r, tm // d, stride=d), :] for c in range(LANE_CHUNKS)]
                out[:, lo:lo + A_WIDTH] = jnp.concatenate(rows_r, axis=1).astype(BF16)

        emit(0, rope(_dot(h, w_ref[:, base:base + A_WIDTH])) * (HEAD_DIM ** -0.5))
        k = rope(_dot(h, w_ref[:, base + A_WIDTH:base + 2 * A_WIDTH]))
        emit(1, k)
        v = _dot(h, w_ref[:, base + 2 * A_WIDTH:base + 3 * A_WIDTH])
        emit(2, v)
        kv.append((k, v))

    ext[hist:hist + tm, :] = _dot(h, w_ref[:, QKV_W:QKV_W + D_POOL])
    if pos0 + 1 >= max(POOL_WINDOWS):
        pos = None
    else:
        pos = pos0 + t * (tm // stride) + lax.broadcasted_iota(jnp.int32, (tm, 1), 0) // stride
    for gi, w in enumerate(POOL_WINDOWS):
        sl = slice(gi * POOL_GC, (gi + 1) * POOL_GC)
        tok = ext[hist:hist + tm, sl]
        acc = tok
        for back in range(1, w):
            lo = hist - back * stride
            acc = acc + ext[lo:lo + tm, sl]
        cnt = float(w) if pos is None else jnp.minimum(w, pos + 1).astype(F32)
        d_tok = acc / cnt - tok
        p_ref[:, sl] = (_dot(d_tok.astype(BF16), pw_ref[gi]) * ps_ref[:, sl]).astype(BF16)
    tail_ref[...] = ext[hist + tm - tail_rows:hist + tm, :]
    _carry_history(ext, hist, tm)

    for g, (k, v) in enumerate(kv):
        def write_rows(g=g, k=k, v=v):
            for dst, lo, width in win_chunks[g]:
                win_refs[g][dst, 0:A_WIDTH, :] = k[lo:lo + width].T
                win_refs[g][dst, A_WIDTH:KV_W, :] = v[lo:lo + width].T

        if win_first[g] == 0:
            write_rows()
        else:
            pl.when(t >= win_first[g])(write_rows)


def _ab_in(x, nw, w_in, cos, sin, pool_w, pool_scale, layer, halo, *, nb, nt, tm, stride, dils, pos0, tail_rows,
           tail_per_tile, win_shapes, win_blocks, win_index, win_first, win_chunks):
    rows = x.shape[0]
    hist = _hist_rows(POOL_STATE, stride)
    assert halo.shape == (nb, hist, D_POOL)
    assert cos.shape == (nt * tm, V7X_LANES)
    table_index = lambda b, t: (t, 0)
    seq_rows = nt * tm
    tail_total, tail_spec = _tail_spec(tail_rows, D_POOL, nt, tail_per_tile)
    qkv_shapes = [jax.ShapeDtypeStruct((nb, seq_rows // d, d * GROUP_W), BF16) for d in dils]
    qkv_specs = [pl.BlockSpec((None, tm // d, d * GROUP_W), lambda b, t: (b, t, 0)) for d in dils]
    vm = (2 * _nbytes((tm, D_MODEL), F32) + _nbytes((D_MODEL, QKV_W + D_POOL), BF16) + 2 * _nbytes((tm, QKV_W), BF16)
          + 2 * sum(_nbytes(blk, F32) for blk in win_blocks) + 4 * _nbytes((tm, D_POOL), F32)
          + _nbytes((2 * hist + tm, D_POOL), F32) + 2 * _nbytes((tail_rows, D_POOL), F32) + 8 * _nbytes((tm, A_WIDTH), F32))
    in_specs = [pl.BlockSpec((tm, D_MODEL), lambda b, t: (b * nt + t, 0)),
                _layer_block((1, D_MODEL), layer), _layer_block((D_MODEL, QKV_W + D_POOL), layer),
                pl.BlockSpec((tm, V7X_LANES), table_index), pl.BlockSpec((tm, V7X_LANES), table_index),
                _layer_block((len(POOL_WINDOWS), POOL_GC, POOL_GC), layer), _layer_block((1, D_POOL), layer),
                pl.BlockSpec((None, hist, D_POOL), lambda b, t: (b, 0, 0))]
    args = [x, nw, w_in, cos, sin, pool_w, pool_scale, halo]
    return pl.pallas_call(
        functools.partial(_ab_in_kernel, tm=tm, stride=stride, dils=dils, pos0=pos0, hist=hist, tail_rows=tail_rows,
                          win_first=win_first, win_chunks=win_chunks),
        grid=(nb, nt),
        in_specs=in_specs,
        out_specs=qkv_specs + [pl.BlockSpec((tm, D_POOL), lambda b, t: (b * nt + t, 0))]
                  + [pl.BlockSpec(blk, idx) for blk, idx in zip(win_blocks, win_index)] + [tail_spec],
        out_shape=qkv_shapes + [jax.ShapeDtypeStruct((rows, D_POOL), BF16)]
                  + [jax.ShapeDtypeStruct(s, F32) for s in win_shapes]
                  + [jax.ShapeDtypeStruct((nb, tail_total, D_POOL), F32)],
        scratch_shapes=[pltpu.VMEM((hist + tm, D_POOL), F32), pltpu.VMEM((LANE_CHUNKS, tm, V7X_LANES), F32)],
        compiler_params=_cparams(2, vm),
        name="ab_in",
    )(*args)


def _attn_kernel(*refs, merge, other_dils, side_seg, side_every, grid_dims):
    refs = list(refs)
    q_ref, kc_ref, kp_ref, vc_ref, vp_ref = refs[:5]
    del refs[:5]
    if merge:
        oa_ref, la_ref, ob_ref, lb_ref = refs[:4]
        del refs[:4]
    if side_seg:
        sq_ref, *c_refs = refs[:1 + N_DIL]
        del refs[:1 + N_DIL]
    out_ref = refs.pop(0)
    if not merge:
        lse_ref = refs.pop(0)
    if side_seg:
        sout_ref = refs.pop(0)
    if merge:
        oa_n, la_n, ob_n, lb_n = refs
    n = pl.program_id(2)
    if side_seg:
        if side_every == 1:
            _attn_s_body(sq_ref, c_refs, sout_ref, side_seg)
        else:
            _, dil, nq = grid_dims
            step = (pl.program_id(0) * dil + pl.program_id(1)) * nq + n
            pl.when(step % side_every == 0)(lambda: _attn_s_body(sq_ref, c_refs, sout_ref, side_seg))
    blk = ATT_BLK
    qi = lax.broadcasted_iota(jnp.int32, (blk, 2 * blk), 0)
    kj = lax.broadcasted_iota(jnp.int32, (blk, 2 * blk), 1)
    band = (kj >= qi) & (kj <= qi + SPAN)
    band_first = band & ((kj >= blk) | (n > 0))
    lane = lax.broadcasted_iota(jnp.int32, (blk, V7X_LANES), 1)
    even_head = lane < HEAD_DIM
    lane_head = lane // LSE_LANES

    if merge:
        for (o_ref, l_ref, o_n, l_n), d in zip(((oa_ref, la_ref, oa_n, la_n), (ob_ref, lb_ref, ob_n, lb_n)), other_dils):
            for r in range(d):
                dst = pl.ds(r, ATT_TQ // d, stride=d)
                for hp in range(LANE_CHUNKS):
                    lo = r * A_WIDTH + hp * V7X_LANES
                    o_n[hp, dst, :] = o_ref[:, lo:lo + V7X_LANES].astype(F32)
                l_n[dst, :] = l_ref[:, r * V7X_LANES:(r + 1) * V7X_LANES]

    for c in range(ATT_TQ // blk):
        rows = slice(c * blk, (c + 1) * blk)
        mask = band_first if c == 0 else band
        mask2 = jnp.concatenate([mask, mask], axis=0)
        lse_blk = jnp.zeros((blk, V7X_LANES), F32)
        outs = []
        for hp in range(A_HEADS // 2):
            ls = slice(hp * V7X_LANES, (hp + 1) * V7X_LANES)
            q2 = q_ref[rows, ls].astype(F32)
            qs = jnp.concatenate([jnp.where(even_head, q2, 0.0), jnp.where(even_head, 0.0, q2)], axis=0).astype(BF16)
            if c == 0:
                k2 = jnp.concatenate([kp_ref[:, ls], kc_ref[0:blk, ls]], axis=0)
                v2 = jnp.concatenate([vp_ref[:, ls], vc_ref[0:blk, ls]], axis=0)
            else:
                k2 = kc_ref[(c - 1) * blk:(c + 1) * blk, ls]
                v2 = vc_ref[(c - 1) * blk:(c + 1) * blk, ls]
            s = jnp.where(mask2, _dot_nt(qs, k2), NEG_INF)
            m = jnp.max(s, axis=-1, keepdims=True)
            p = jnp.exp(s - m)
            den = jnp.sum(p, axis=-1, keepdims=True)
            o = _dot(p.astype(BF16), v2) / den
            lse = m + jnp.log(den)
            outs.append(jnp.where(even_head, o[0:blk], o[blk:2 * blk]))
            lse_blk = jnp.where(lane_head == 2 * hp, lse[0:blk],
                                jnp.where(lane_head == 2 * hp + 1, lse[blk:2 * blk], lse_blk))
        if not merge:
            for hp, o2 in enumerate(outs):
                out_ref[rows, hp * V7X_LANES:(hp + 1) * V7X_LANES] = o2.astype(BF16)
            lse_ref[rows, :] = lse_blk
            continue
        la, lb = la_n[rows, :], lb_n[rows, :]
        mx = jnp.maximum(lse_blk, jnp.maximum(la, lb))
        e1, ea, eb = jnp.exp(lse_blk - mx), jnp.exp(la - mx), jnp.exp(lb - mx)
        tot = e1 + ea + eb
        w1, wa, wb = e1 / tot, ea / tot, eb / tot
        for hp, o2 in enumerate(outs):
            ls = slice(hp * V7X_LANES, (hp + 1) * V7X_LANES)
            ce, co = 2 * hp * LSE_LANES, (2 * hp + 1) * LSE_LANES
            spread = lambda w: jnp.where(even_head, w[:, ce:ce + 1], w[:, co:co + 1])
            a2 = spread(w1) * o2 + spread(wa) * oa_n[hp, rows, :] + spread(wb) * ob_n[hp, rows, :]
            out_ref[rows, ls] = a2.astype(BF16)


def _attn_group(qkv_g, g, nb, seq, others=None, side=None):
    dil = DIL_PATTERNS[g][1]
    sub = seq // dil
    tq, blk = ATT_TQ, ATT_BLK
    merge = others is not None
    nq = sub // tq
    parts = GROUP_W // A_WIDTH
    cur = lambda part: pl.BlockSpec((None, tq, A_WIDTH), lambda b, r, n: (b, n, r * parts + part))
    prev = lambda part: pl.BlockSpec((None, blk, A_WIDTH),
                                     lambda b, r, n: (b, jnp.maximum(n * (tq // blk) - 1, 0), r * parts + part))
    in_specs = [cur(0), cur(1), prev(1), cur(2), prev(2)]
    args = [qkv_g] * 5
    o_spec = pl.BlockSpec((None, tq, A_WIDTH), lambda b, r, n: (b, n, r))
    l_spec = pl.BlockSpec((None, tq, V7X_LANES), lambda b, r, n: (b, n, r))
    o_shape = jax.ShapeDtypeStruct((nb, sub, dil * A_WIDTH), BF16)
    scratch = []
    other_dils = ()
    if merge:
        assert dil == 1
        for o, lse, d in others:
            in_specs += [pl.BlockSpec((None, tq // d, d * A_WIDTH), lambda b, r, n: (b, n, 0)),
                         pl.BlockSpec((None, tq // d, d * V7X_LANES), lambda b, r, n: (b, n, 0))]
            args += [o, lse]
            scratch += [pltpu.VMEM((LANE_CHUNKS, tq, V7X_LANES), F32), pltpu.VMEM((tq, V7X_LANES), F32)]
        other_dils = tuple(d for _, _, d in others)
        out_specs, out_shape = o_spec, o_shape
    else:
        out_specs = [o_spec, l_spec]
        out_shape = [o_shape, jax.ShapeDtypeStruct((nb, sub, dil * V7X_LANES), F32)]
    vm = (2 * (10 * _nbytes((tq, A_WIDTH), BF16) + 3 * _nbytes((tq, V7X_LANES), F32))
          + 2 * _nbytes((tq, A_WIDTH + V7X_LANES), F32))
    side_seg, every = 0, 1
    if side is not None:
        s_qkv, caches_t, jc, first_seq, n_seq, side_seg = side
        steps = nb * dil * nq
        n = max(n_seq // steps, 1)
        every = max(steps // n_seq, 1)
        assert n * steps == n_seq * every and first_seq % n == 0 and first_seq + n_seq <= caches_t[0].shape[1]
        host = lambda b, r, i: ((b * dil + r) * nq + i) // every
        in_specs += ([pl.BlockSpec((n * side_seg, QKV_W), lambda b, r, i: (first_seq // n + host(b, r, i), 0))]
                     + [pl.BlockSpec((None, n, KV_W, c.shape[3]),
                                     lambda b, r, i: (jc, first_seq // n + host(b, r, i), 0, 0)) for c in caches_t])
        args += [s_qkv, *caches_t]
        s_spec = pl.BlockSpec((n * side_seg, A_WIDTH), lambda b, r, i: (host(b, r, i), 0))
        s_shape = jax.ShapeDtypeStruct((n_seq * side_seg, A_WIDTH), F32)
        out_specs, out_shape = ([out_specs, s_spec], [out_shape, s_shape]) if merge else (out_specs + [s_spec], out_shape + [s_shape])
        cache_bytes = n * sum(_nbytes(c.shape[2:], F32) for c in caches_t)
        vm += (2 * (cache_bytes + _nbytes((n * side_seg, QKV_W), F32))
               + 16 * _nbytes((A_HEADS * side_seg, ATT_S_CHUNK), F32) + 4 * _nbytes((A_WIDTH, ATT_S_CHUNK), F32))
    res = pl.pallas_call(
        functools.partial(_attn_kernel, merge=merge, other_dils=other_dils, side_seg=side_seg, side_every=every,
                          grid_dims=(nb, dil, nq)),
        grid=(nb, dil, nq),
        in_specs=in_specs,
        out_specs=out_specs,
        out_shape=out_shape,
        scratch_shapes=scratch,
        compiler_params=_cparams(3, vm),
        name=f"attn_d{dil}",
    )(*args)
    if merge and side is not None:
        return res[0].reshape(nb * seq, A_WIDTH), res[1]
    if merge:
        return res.reshape(nb * seq, A_WIDTH)
    return tuple(res)


def _attn_s_body(qkv_ref, c_refs, out_ref, seg):
    hq = A_HEADS * seg
    row = lax.broadcasted_iota(jnp.int32, (hq, A_WIDTH), 0)
    lane = lax.broadcasted_iota(jnp.int32, (hq, A_WIDTH), 1)
    head_lanes = (lane // HEAD_DIM) == (row // seg)
    out_lane = lax.broadcasted_iota(jnp.int32, (seg, A_WIDTH), 1) // HEAD_DIM
    pad = V7X_LANES - seg
    for s in range(c_refs[0].shape[0]):
        new = qkv_ref[s * seg:(s + 1) * seg, :]
        scores, values = [], []
        for g, ((win, dil), c_ref) in enumerate(zip(DIL_PATTERNS, c_refs)):
            base = g * GROUP_W
            q = new[:, base:base + A_WIDTH]
            qrows = jnp.where(head_lanes, jnp.concatenate([q] * A_HEADS, axis=0), 0.0).astype(BF16)
            nk = c_ref.shape[2]
            ch = min(nk, ATT_S_CHUNK)
            tok = lax.broadcasted_iota(jnp.int32, (hq, ch), 0) % seg
            key = lax.broadcasted_iota(jnp.int32, (hq, ch), 1)
            same_class = (key % dil) == (tok % dil)
            for lo in range(0, nk, ch):
                valid = same_class & (key >= tok) if lo < seg else same_class
                kt = c_ref[s, 0:A_WIDTH, lo:lo + ch].astype(BF16)
                scores.append(jnp.where(valid, _dot(qrows, kt), NEG_INF))
                values.append((c_ref, lo, ch))
            kn = jnp.concatenate([new[:, base + A_WIDTH:base + 2 * A_WIDTH], jnp.zeros((pad, A_WIDTH), F32)], axis=0)
            vn = jnp.concatenate([new[:, base + 2 * A_WIDTH:base + 3 * A_WIDTH], jnp.zeros((pad, A_WIDTH), F32)], axis=0)
            tok_n = lax.broadcasted_iota(jnp.int32, (hq, V7X_LANES), 0) % seg
            key_n = lax.broadcasted_iota(jnp.int32, (hq, V7X_LANES), 1)
            valid_n = (key_n <= tok_n) & (((tok_n - key_n) % dil) == 0)
            scores.append(jnp.where(valid_n, _dot_nt(qrows, kn.astype(BF16)), NEG_INF))
            values.append(vn.astype(BF16))
        m = scores[0].max(axis=-1, keepdims=True)
        for sc in scores[1:]:
            m = jnp.maximum(m, sc.max(axis=-1, keepdims=True))
        den = jnp.zeros((hq, 1), F32)
        acc = jnp.zeros((hq, A_WIDTH), F32)
        for sc, val in zip(scores, values):
            p = jnp.exp(sc - m)
            den = den + p.sum(axis=-1, keepdims=True)
            if isinstance(val, tuple):
                c_ref, lo, ch = val
                acc = acc + _dot_nt(p.astype(BF16), c_ref[s, A_WIDTH:KV_W, lo:lo + ch].astype(BF16))
            else:
                acc = acc + _dot(p.astype(BF16), val)
        acc = acc / den
        o = jnp.zeros((seg, A_WIDTH), F32)
        for hh in range(A_HEADS):
            o = jnp.where(out_lane == hh, acc[hh * seg:(hh + 1) * seg], o)
        out_ref[s * seg:(s + 1) * seg, :] = o


def _rope_tables(pos):
    half = HEAD_DIM // 2
    inv = jnp.power(ROPE_THETA, -jnp.arange(half, dtype=F32) / half)
    reps = V7X_LANES // half
    ang = pos.astype(F32)[:, None] * jnp.tile(inv, reps)[None, :]
    sign = jnp.tile(jnp.concatenate([-jnp.ones((half,), F32), jnp.ones((half,), F32)]), reps // 2)
    return jnp.cos(ang), jnp.sin(ang) * sign[None, :]


def _prompt_mixers(w, nb, seq):
    tm = MIXER_ROW_TILE
    nt = seq // tm
    cos, sin = _rope_tables(jnp.arange(seq, dtype=jnp.int32))
    dils = tuple(d for _, d in DIL_PATTERNS)
    win_tot = tuple(min(win, seq) for win, _ in DIL_PATTERNS)
    win_r = tuple(min(wt, tm) for wt in win_tot)
    win_first = tuple(nt - max(wt // tm, 1) for wt in win_tot)
    win_cfg = dict(
        win_shapes=[(nb, KV_W, wt) for wt in win_tot],
        win_blocks=[(1, KV_W, r) for r in win_r],
        win_index=[(lambda b, t, first=first: (b, 0, jnp.maximum(t - first, 0))) for first in win_first],
        win_first=win_first,
        win_chunks=tuple(((0, tm - r, r),) for r in win_r))
    pool_halo = jnp.zeros((nb, _hist_rows(POOL_STATE, 1), D_POOL), F32)
    conv_halo = jnp.zeros((nb, _hist_rows(CONV_W - 1, 1), D_MODEL), F32)

    def ab_mixer(x, j, sample):
        q0, q1, q2, p, w0, w1, w2, ptail = _ab_in_call(
            x, w, j, cos, sin, pool_halo, nb=nb, nt=nt, tm=tm, stride=1, dils=dils, pos0=0,
            tail_rows=pool_halo.shape[1], tail_per_tile=False, **win_cfg)
        s_qkv, caches_t, nseq, seg = sample
        quarter = nseq // 4
        ob, lb, att0 = _attn_group(q2, 2, nb, seq, side=(s_qkv, caches_t, j, 0, quarter, seg))
        oa, la, att1 = _attn_group(q1, 1, nb, seq, side=(s_qkv, caches_t, j, quarter, quarter, seg))
        a, att2 = _attn_group(q0, 0, nb, seq, others=((oa, la, dils[1]), (ob, lb, dils[2])),
                              side=(s_qkv, caches_t, j, 2 * quarter, nseq - 2 * quarter, seg))
        return (a, p, (w0, w1, w2), ptail[:, ptail.shape[1] - POOL_STATE:],
                jnp.concatenate([att0, att1, att2], axis=0))

    def conv_mixer(x, j):
        x, ctail = _conv_layer(x, w["mix_norm_c"], w["conv_w_in"], w["conv_w"], w["conv_w_out"], j, conv_halo,
                               nb=nb, nt=nt, tm=tm, stride=1, tail_rows=conv_halo.shape[1], tail_per_tile=False)
        return x, ctail[:, ctail.shape[1] - (CONV_W - 1):]

    def present_windows(wins):
        out = []
        for g in range(N_DIL):
            wt = jnp.stack(wins[g], axis=0).reshape(len(wins[g]), nb, 2, A_HEADS, HEAD_DIM, win_tot[g])
            out.append(jnp.transpose(wt, (0, 1, 5, 2, 3, 4)))
        return out

    return ab_mixer, conv_mixer, present_windows


def _ab_in_call(x, w, j, cos, sin, halo, **kw):
    return _ab_in(x, w["mix_norm_ab"], w["ab_w_in"], cos, sin, w["pool_w"], w["pool_scale"], j, halo, **kw)


def _sample_mixers(w, nseq, seg, pool_state, conv_state):
    tm = SAMPLE_ROW_TILE
    rows = seg * nseq
    nt = rows // tm
    tok_tile = tm // nseq
    cos, sin = _rope_tables(PAST_LEN + jnp.arange(seg, dtype=jnp.int32))
    cos, sin = jnp.repeat(cos, nseq, axis=0), jnp.repeat(sin, nseq, axis=0)
    win_cfg = dict(
        win_shapes=[(seg, KV_W, nseq)] * N_DIL,
        win_blocks=[(tok_tile, KV_W, nseq)] * N_DIL,
        win_index=[lambda b, t: (t, 0, 0)] * N_DIL,
        win_first=(0,) * N_DIL,
        win_chunks=(tuple((i, i * nseq, nseq) for i in range(tok_tile)),) * N_DIL)

    def ab_in(x, j):
        state = jnp.transpose(pool_state[j], (1, 0, 2))
        halo = state.reshape(1, POOL_STATE * nseq, D_POOL)
        q0, q1, q2, p, w0, w1, w2, u = _ab_in_call(
            x, w, j, cos, sin, halo, nb=1, nt=nt, tm=tm, stride=nseq, dils=(1,) * N_DIL, pos0=PAST_LEN,
            tail_rows=tm, tail_per_tile=True, **win_cfg)
        qkv = jnp.concatenate([q.reshape(seg, nseq, GROUP_W) for q in (q0, q1, q2)], axis=-1)
        qkv = jnp.transpose(qkv, (1, 0, 2)).reshape(rows, QKV_W).astype(F32)
        u = u.reshape(seg, nseq, D_POOL)
        pool = jnp.transpose(jnp.concatenate([state, u], axis=0)[-POOL_STATE:], (1, 0, 2))
        return qkv, p, (w0, w1, w2), pool

    def attn_rows(a):
        return jnp.transpose(a.reshape(nseq, seg, A_WIDTH), (1, 0, 2)).reshape(rows, A_WIDTH).astype(BF16)

    def conv_mixer(x, j):
        state = jnp.transpose(conv_state[j], (1, 0, 2))
        halo = state.reshape(1, (CONV_W - 1) * nseq, D_MODEL)
        x, cu = _conv_layer(x, w["mix_norm_c"], w["conv_w_in"], w["conv_w"], w["conv_w_out"], j, halo,
                            nb=1, nt=nt, tm=tm, stride=nseq, tail_rows=tm, tail_per_tile=True)
        cu = cu.reshape(seg, nseq, D_MODEL)
        return x, jnp.transpose(jnp.concatenate([state, cu], axis=0)[-(CONV_W - 1):], (1, 0, 2))

    def present_windows(wins):
        out = []
        for g in range(N_DIL):
            wt = jnp.stack(wins[g], axis=0).reshape(len(wins[g]), seg, 2, A_HEADS, HEAD_DIM, nseq)
            out.append(jnp.transpose(wt, (0, 5, 1, 2, 3, 4)))
        return out

    return ab_in, attn_rows, conv_mixer, present_windows


def _trunks(xp, xs, w, caches, pool_state, conv_state, *, nb, seq, nseq, seg):
    depth = w["ffn1_norm"].shape[0]
    p_ab, p_conv, p_windows = _prompt_mixers(w, nb, seq)
    s_ab_in, s_attn_rows, s_conv, s_windows = _sample_mixers(w, nseq, seg, pool_state, conv_state)
    caches_t = [jnp.transpose(c, (0, 1, 3, 4, 5, 2)).reshape(c.shape[0], nseq, KV_W, c.shape[2]) for c in caches]
    ffn1 = lambda x, l, **kw: _ffn(x, w["ffn1_norm"], w["ffn1_w_gu"], w["ffn1_w_down"], l, **kw)
    ffn2 = lambda x, l, **kw: _ffn(x, w["ffn2_norm"], w["ffn2_w_gu"], w["ffn2_w_down"], l,
                                   final_w=w["final_norm"] if l == depth - 1 else None, **kw)
    wins_p, wins_s = [[] for _ in range(N_DIL)], [[] for _ in range(N_DIL)]
    pools_p, pools_s, convs_p, convs_s = [], [], [], []
    for l in range(depth):
        j = l // 2
        if l % 2 == 0:
            xs = ffn1(xs, l)
            qkv_s, p_s, win_s, pool_s = s_ab_in(xs, j)
            xp = ffn1(xp, l)
            a_p, p_p, win_p, pool_p, att_s = p_ab(xp, j, (qkv_s, caches_t, nseq, seg))
            xp = ffn2(xp, l, premix=(a_p, p_p, w["ab_w_out"], j))
            xs = ffn2(xs, l, premix=(s_attn_rows(att_s), p_s, w["ab_w_out"], j))
            for g in range(N_DIL):
                wins_p[g].append(win_p[g])
                wins_s[g].append(win_s[g])
            pools_p.append(pool_p)
            pools_s.append(pool_s)
        else:
            xp, conv_p = p_conv(ffn1(xp, l), j)
            xs, conv_s = s_conv(ffn1(xs, l), j)
            xp, xs = ffn2(xp, l), ffn2(xs, l)
            convs_p.append(conv_p)
            convs_s.append(conv_s)
    return ((xp, p_windows(wins_p), jnp.stack(pools_p, axis=0), jnp.stack(convs_p, axis=0)),
            (xs, s_windows(wins_s), jnp.stack(pools_s, axis=0), jnp.stack(convs_s, axis=0)))


def kernel(x_prompt, x_sample, cache_win0, cache_win1, cache_win2, state_pool, state_conv, ffn1_norm, ffn1_w_gu, ffn1_w_down, mix_norm, ffn2_norm, ffn2_w_gu, ffn2_w_down, ab_w_in, ab_w_out, pool_w, pool_scale, conv_w_in, conv_w, conv_w_out, final_norm):
    batch, seq, _ = x_prompt.shape
    dbatch, dseq, _ = x_sample.shape
    assert seq % ROW_TILE == 0 and (dbatch * dseq) % ROW_TILE == 0 and dbatch % 2 == 0
    assert SAMPLE_ROW_TILE % dbatch == 0 and (dbatch * dseq) % SAMPLE_ROW_TILE == 0 and dbatch % V7X_SUBLANES == 0
    assert all((seq // dil) % ATT_TQ == 0 and ROW_TILE % dil == 0 for _, dil in DIL_PATTERNS)
    assert dseq == V7X_SUBLANES and PAST_LEN + 1 >= max(POOL_WINDOWS) and PAST_LEN >= POOL_STATE
    assert all(c.shape[2] == win and win == dil * SPAN for c, (win, dil) in zip((cache_win0, cache_win1, cache_win2), DIL_PATTERNS))
    row3 = lambda a: a.reshape(a.shape[0], 1, a.shape[1])
    w = dict(
        ffn1_norm=row3(ffn1_norm), ffn1_w_gu=ffn1_w_gu.astype(BF16), ffn1_w_down=ffn1_w_down.astype(BF16),
        ffn2_norm=row3(ffn2_norm), ffn2_w_gu=ffn2_w_gu.astype(BF16), ffn2_w_down=ffn2_w_down.astype(BF16),
        mix_norm_ab=row3(mix_norm[0::2]), mix_norm_c=row3(mix_norm[1::2]),
        ab_w_in=ab_w_in.astype(BF16), ab_w_out=ab_w_out.astype(BF16), pool_w=pool_w.astype(BF16), pool_scale=row3(pool_scale),
        conv_w_in=conv_w_in.astype(BF16), conv_w=conv_w, conv_w_out=conv_w_out.astype(BF16), final_norm=final_norm)
    xs = jnp.transpose(x_sample, (1, 0, 2)).reshape(dseq * dbatch, D_MODEL)
    (y_p, win_p, pool_p, conv_p), (y_s, win_s, pool_s, conv_s) = _trunks(
        x_prompt.reshape(batch * seq, D_MODEL), xs, w, (cache_win0, cache_win1, cache_win2), state_pool, state_conv,
        nb=batch, seq=seq, nseq=dbatch, seg=dseq)
    y_s = jnp.transpose(y_s.reshape(dseq, dbatch, D_MODEL), (1, 0, 2))
    return (y_p.reshape(batch, seq, D_MODEL), y_s,
            win_p[0], win_p[1], win_p[2], pool_p, conv_p,
            win_s[0], win_s[1], win_s[2], pool_s, conv_s)
```

```python
import functools

import jax
import jax.numpy as jnp
from jax import lax
from jax.experimental import pallas as pl
from jax.experimental.pallas import tpu as pltpu

F32 = jnp.float32
BF16 = jnp.bfloat16

D_MODEL = 1024
HEAD_DIM = 64
A_HEADS = 8
A_WIDTH = A_HEADS * HEAD_DIM
DIL_PATTERNS = ((128, 1), (512, 4), (2048, 16))
N_DIL = len(DIL_PATTERNS)
SPAN = 128
GROUP_W = 3 * A_WIDTH
QKV_W = N_DIL * GROUP_W
KV_W = 2 * A_WIDTH
POOL_WINDOWS = (2, 4, 8, 16)
D_POOL = 512
POOL_GC = 128
POOL_STATE = 15
CONV_W = 3
D_FF = 2816
ROPE_THETA = 10000.0
EPS = 1e-6
PAST_LEN = 2048

V7X_LANES = 128
V7X_SUBLANES = 8
V7X_MXU_DIM = 256
V7X_VMEM_USABLE = 56 * 1024 * 1024
MIB = 1024 * 1024

ROW_TILE = 512
MIXER_ROW_TILE = 512
SAMPLE_ROW_TILE = 256
ATT_S_CHUNK = 512
ATT_TQ = 256
ATT_BLK = SPAN
LSE_LANES = V7X_LANES // A_HEADS
LANE_CHUNKS = A_WIDTH // V7X_LANES
NEG_INF = float("-inf")


def _round_up(n, m):
    return -(-n // m) * m


def _nbytes(shape, dtype):
    n = 1
    for s in shape:
        n *= s
    return n * jnp.dtype(dtype).itemsize


def _cparams(n_axes, block_bytes):
    limit = int(min(V7X_VMEM_USABLE, block_bytes * 1.25 + 12 * MIB))
    return pltpu.CompilerParams(dimension_semantics=("arbitrary",) * n_axes, vmem_limit_bytes=limit)


def _layer_block(shape, layer):
    nd = len(shape)
    return pl.BlockSpec((None,) + tuple(shape), lambda *_: (layer,) + (0,) * nd, pipeline_mode=pl.Buffered(1))


def _const_block(shape):
    nd = len(shape)
    return pl.BlockSpec(tuple(shape), lambda *_: (0,) * nd, pipeline_mode=pl.Buffered(1))


def _rms(x, w):
    return x * lax.rsqrt(jnp.mean(x * x, axis=-1, keepdims=True) + EPS) * w


def _dot(a, b):
    return jnp.dot(a, b, preferred_element_type=F32)


def _dot_nt(a, b):
    return lax.dot_general(a, b, (((1,), (1,)), ((), ())), preferred_element_type=F32)


def _merge_groups(group_refs, dils, a_scr, regroup, tm):
    blk = ATT_BLK
    lane = lax.broadcasted_iota(jnp.int32, (blk, V7X_LANES), 1)
    even_head = lane < HEAD_DIM
    for (o_ref, l_ref), d, (o_n, l_n) in zip(group_refs, dils, regroup):
        for r in range(d):
            dst = pl.ds(r, tm // d, stride=d) if d > 1 else slice(0, tm)
            for hp in range(LANE_CHUNKS):
                lo = r * A_WIDTH + hp * V7X_LANES
                o_n[hp, dst, :] = o_ref[:, lo:lo + V7X_LANES].astype(F32)
            l_n[dst, :] = l_ref[:, r * V7X_LANES:(r + 1) * V7X_LANES]
    for c in range(tm // blk):
        rows = slice(c * blk, (c + 1) * blk)
        lses = [l_n[rows, :] for _, l_n in regroup]
        mx = functools.reduce(jnp.maximum, lses)
        es = [jnp.exp(l - mx) for l in lses]
        tot = functools.reduce(jnp.add, es)
        ws = [e / tot for e in es]
        for hp in range(LANE_CHUNKS):
            ce, co = 2 * hp * LSE_LANES, (2 * hp + 1) * LSE_LANES
            acc = None
            for w, (o_n, _) in zip(ws, regroup):
                term = jnp.where(even_head, w[:, ce:ce + 1], w[:, co:co + 1]) * o_n[hp, rows, :]
                acc = term if acc is None else acc + term
            a_scr[rows, hp * V7X_LANES:(hp + 1) * V7X_LANES] = acc.astype(BF16)


def _ffn_kernel(*refs, premix, final, merge_dils, tm):
    refs = list(refs)
    x_ref, nw_ref, wgu_ref, wd_ref = refs[:4]
    del refs[:4]
    x = x_ref[...]
    if premix == "rows":
        a_ref, p_ref, wo_ref = refs[:3]
        del refs[:3]
        x = x + _dot(a_ref[...], wo_ref[0:A_WIDTH, :]) + _dot(p_ref[...], wo_ref[A_WIDTH:A_WIDTH + D_POOL, :])
    elif premix == "merge":
        ng = len(merge_dils)
        cur_refs = [(refs[2 * g], refs[2 * g + 1]) for g in range(ng)]
        next_refs = [(refs[2 * ng + 2 * g], refs[2 * ng + 2 * g + 1]) for g in range(ng)]
        p_ref, wo_ref = refs[4 * ng:4 * ng + 2]
        del refs[:4 * ng + 2]
        mix_scr = refs[-1 - 2 * ng]
        regroup = [(refs[-2 * ng + 2 * g], refs[-2 * ng + 2 * g + 1]) for g in range(ng)]
        del refs[-1 - 2 * ng:]
        pl.when(pl.program_id(0) == 0)(lambda: _merge_groups(cur_refs, merge_dils, mix_scr, regroup, tm))
        x = x + _dot(mix_scr[...], wo_ref[0:A_WIDTH, :]) + _dot(p_ref[...], wo_ref[A_WIDTH:A_WIDTH + D_POOL, :])
    if final:
        fw_ref = refs.pop(0)
    o_ref, a_scr = refs
    h = _rms(x, nw_ref[...]).astype(BF16)
    for lo in range(0, D_FF, V7X_MXU_DIM):
        g = _dot(h, wgu_ref[:, lo:lo + V7X_MXU_DIM])
        u = _dot(h, wgu_ref[:, D_FF + lo:D_FF + lo + V7X_MXU_DIM])
        a_scr[:, lo:lo + V7X_MXU_DIM] = (g * jax.nn.sigmoid(g) * u).astype(BF16)
    y = x + 0.5 * _dot(a_scr[...], wd_ref[...])
    if final:
        y = _rms(y, fw_ref[...])
    o_ref[...] = y
    if premix == "merge":
        _merge_groups(next_refs, merge_dils, mix_scr, regroup, tm)


def _ffn(x, nw, wgu, wd, layer, *, final_w=None, premix=None):
    rows = x.shape[0]
    tm = ROW_TILE
    final = final_w is not None
    in_specs = [pl.BlockSpec((tm, D_MODEL), lambda i: (i, 0)),
                _layer_block((1, D_MODEL), layer), _layer_block((D_MODEL, 2 * D_FF), layer),
                _layer_block((D_FF, D_MODEL), layer)]
    args = [x, nw, wgu, wd]
    vm = (4 * _nbytes((tm, D_MODEL), F32) + _nbytes((D_MODEL, 2 * D_FF), BF16) + _nbytes((D_FF, D_MODEL), BF16)
          + _nbytes((tm, D_FF), BF16))
    scratch = [pltpu.VMEM((tm, D_FF), BF16)]
    mode, merge_dils = None, ()
    if premix is not None:
        a, p, w_out, j = premix
        if isinstance(a, (list, tuple)):
            mode, merge_dils = "merge", tuple(d for _, _, d in a)
            scratch.append(pltpu.VMEM((tm, A_WIDTH), BF16))
            last = rows // tm - 1
            for index in (lambda i: (i, 0), lambda i: (jnp.minimum(i + 1, last), 0)):
                for o, lse, d in a:
                    assert tm % (d * 2 * V7X_SUBLANES) == 0
                    in_specs += [pl.BlockSpec((tm // d, d * A_WIDTH), index),
                                 pl.BlockSpec((tm // d, d * V7X_LANES), index)]
                    args += [o.reshape(rows // d, d * A_WIDTH), lse.reshape(rows // d, d * V7X_LANES)]
                    vm += 2 * _nbytes((tm, A_WIDTH), BF16) + 2 * _nbytes((tm, V7X_LANES), F32)
            for _ in a:
                scratch += [pltpu.VMEM((LANE_CHUNKS, tm, V7X_LANES), F32), pltpu.VMEM((tm, V7X_LANES), F32)]
                vm += _nbytes((tm, A_WIDTH + V7X_LANES), F32)
        else:
            mode = "rows"
            in_specs.append(pl.BlockSpec((tm, A_WIDTH), lambda i: (i, 0)))
            args.append(a)
        in_specs += [pl.BlockSpec((tm, D_POOL), lambda i: (i, 0)), _layer_block((A_WIDTH + D_POOL, D_MODEL), j)]
        args += [p, w_out]
        vm += 4 * _nbytes((tm, A_WIDTH), BF16) + _nbytes((D_MODEL, D_MODEL), BF16)
    if final:
        in_specs.append(_const_block((1, D_MODEL)))
        args.append(final_w.reshape(1, D_MODEL))
    return pl.pallas_call(
        functools.partial(_ffn_kernel, premix=mode, final=final, merge_dils=merge_dils, tm=tm),
        grid=(rows // tm,),
        in_specs=in_specs,
        out_specs=pl.BlockSpec((tm, D_MODEL), lambda i: (i, 0)),
        out_shape=jax.ShapeDtypeStruct((rows, D_MODEL), F32),
        scratch_shapes=scratch,
        compiler_params=_cparams(1, vm),
        name="ffn" + {None: "", "rows": "_mix", "merge": "_merge"}[mode] + ("_final" if final else ""),
    )(*args)


def _hist_rows(positions, stride):
    return _round_up(positions * stride, V7X_SUBLANES)


def _init_history(ext, halo_ref, hist):
    @pl.when(pl.program_id(1) == 0)
    def _():
        ext[0:hist, :] = halo_ref[...]


def _carry_history(ext, hist, tm):
    ext[0:hist, :] = ext[tm:tm + hist, :]


def _conv_kernel(x_ref, nw_ref, win_ref, cw_ref, wout_ref, halo_ref, o_ref, tail_ref, ext, *, tm, stride, hist, tail_rows):
    _init_history(ext, halo_ref, hist)
    x = x_ref[...]
    h = _rms(x, nw_ref[...]).astype(BF16)
    gb = _dot(h, win_ref[:, 0:D_MODEL])
    gc = _dot(h, win_ref[:, D_MODEL:2 * D_MODEL])
    v = _dot(h, win_ref[:, 2 * D_MODEL:3 * D_MODEL])
    ext[hist:hist + tm, :] = gc * v
    cw = cw_ref[...]
    y = cw[CONV_W - 1] * ext[hist:hist + tm, :]
    for back in range(1, CONV_W):
        lo = hist - back * stride
        y = y + cw[CONV_W - 1 - back] * ext[lo:lo + tm, :]
    o_ref[...] = x + _dot((gb * y).astype(BF16), wout_ref[...])
    tail_ref[...] = ext[hist + tm - tail_rows:hist + tm, :]
    _carry_history(ext, hist, tm)


def _tail_spec(tail_rows, width, nt, per_tile):
    if per_tile:
        return nt * tail_rows, pl.BlockSpec((None, tail_rows, width), lambda b, t: (b, t, 0))
    return tail_rows, pl.BlockSpec((None, tail_rows, width), lambda b, t: (b, 0, 0))


def _conv_layer(x, nw, w_in, cw, w_out, layer, halo, *, nb, nt, tm, stride, tail_rows, tail_per_tile):
    hist = _hist_rows(CONV_W - 1, stride)
    assert halo.shape == (nb, hist, D_MODEL)
    rows = x.shape[0]
    tail_total, tail_spec = _tail_spec(tail_rows, D_MODEL, nt, tail_per_tile)
    vm = (8 * _nbytes((tm, D_MODEL), F32) + _nbytes((D_MODEL, 4 * D_MODEL), BF16)
          + _nbytes((hist + tm, D_MODEL), F32) + 4 * _nbytes((hist + tail_rows, D_MODEL), F32))
    return pl.pallas_call(
        functools.partial(_conv_kernel, tm=tm, stride=stride, hist=hist, tail_rows=tail_rows),
        grid=(nb, nt),
        in_specs=[pl.BlockSpec((tm, D_MODEL), lambda b, t: (b * nt + t, 0)),
                  _layer_block((1, D_MODEL), layer), _layer_block((D_MODEL, 3 * D_MODEL), layer),
                  _layer_block((CONV_W, D_MODEL), layer), _layer_block((D_MODEL, D_MODEL), layer),
                  pl.BlockSpec((None, hist, D_MODEL), lambda b, t: (b, 0, 0))],
        out_specs=[pl.BlockSpec((tm, D_MODEL), lambda b, t: (b * nt + t, 0)), tail_spec],
        out_shape=[jax.ShapeDtypeStruct((rows, D_MODEL), F32),
                   jax.ShapeDtypeStruct((nb, tail_total, D_MODEL), F32)],
        scratch_shapes=[pltpu.VMEM((hist + tm, D_MODEL), F32)],
        compiler_params=_cparams(2, vm),
        name="conv_layer",
    )(x, nw, w_in, cw, w_out, halo)


def _ab_in_kernel(x_ref, nw_ref, w_ref, cos_ref, sin_ref, pw_ref, ps_ref, halo_ref,
                  q0_ref, q1_ref, q2_ref, p_ref, win0_ref, win1_ref, win2_ref, tail_ref, ext, perm,
                  *, tm, stride, dils, pos0, hist, tail_rows, win_first, win_chunks):
    qkv_refs = (q0_ref, q1_ref, q2_ref)
    win_refs = (win0_ref, win1_ref, win2_ref)
    t = pl.program_id(1)
    _init_history(ext, halo_ref, hist)
    h = _rms(x_ref[...], nw_ref[...]).astype(BF16)

    cos = jnp.concatenate([cos_ref[...]] * (A_WIDTH // V7X_LANES), axis=1)
    sin = jnp.concatenate([sin_ref[...]] * (A_WIDTH // V7X_LANES), axis=1)
    lane = lax.broadcasted_iota(jnp.int32, (tm, A_WIDTH), 1)
    first_half = (lane % HEAD_DIM) < (HEAD_DIM // 2)

    def rope(z):
        partner = jnp.where(first_half, pltpu.roll(z, A_WIDTH - HEAD_DIM // 2, 1), pltpu.roll(z, HEAD_DIM // 2, 1))
        return z * cos + partner * sin

    kv = []
    for g in range(N_DIL):
        base, d = g * GROUP_W, dils[g]

        def emit(part, z, g=g, d=d):
            out = qkv_refs[g]
            if d == 1:
                out[:, part * A_WIDTH:(part + 1) * A_WIDTH] = z.astype(BF16)
                return
            for c in range(LANE_CHUNKS):
                perm[c] = z[:, c * V7X_LANES:(c + 1) * V7X_LANES]
            for r in range(d):
                lo = r * GROUP_W + part * A_WIDTH
                rows_r = [perm[c, pl.ds(r, tm // d, stride=d), :] for c in range(LANE_CHUNKS)]
                out[:, lo:lo + A_WIDTH] = jnp.concatenate(rows_r, axis=1).astype(BF16)

        emit(0, rope(_dot(h, w_ref[:, base:base + A_WIDTH])) * (HEAD_DIM ** -0.5))
        k = rope(_dot(h, w_ref[:, base + A_WIDTH:base + 2 * A_WIDTH]))
        emit(1, k)
        v = _dot(h, w_ref[:, base + 2 * A_WIDTH:base + 3 * A_WIDTH])
        emit(2, v)
        kv.append((k, v))

    ext[hist:hist + tm, :] = _dot(h, w_ref[:, QKV_W:QKV_W + D_POOL])
    if pos0 + 1 >= max(POOL_WINDOWS):
        pos = None
    else:
        pos = pos0 + t * (tm // stride) + lax.broadcasted_iota(jnp.int32, (tm, 1), 0) // stride
    for gi, w in enumerate(POOL_WINDOWS):
        sl = slice(gi * POOL_GC, (gi + 1) * POOL_GC)
        tok = ext[hist:hist + tm, sl]
        acc = tok
        for back in range(1, w):
            lo = hist - back * stride
            acc = acc + ext[lo:lo + tm, sl]
        cnt = float(w) if pos is None else jnp.minimum(w, pos + 1).astype(F32)
        d_tok = acc / cnt - tok
        p_ref[:, sl] = (_dot(d_tok.astype(BF16), pw_ref[gi]) * ps_ref[:, sl]).astype(BF16)
    tail_ref[...] = ext[hist + tm - tail_rows:hist + tm, :]
    _carry_history(ext, hist, tm)

    for g, (k, v) in enumerate(kv):
        def write_rows(g=g, k=k, v=v):
            for dst, lo, width in win_chunks[g]:
                win_refs[g][dst, 0:A_WIDTH, :] = k[lo:lo + width].T
                win_refs[g][dst, A_WIDTH:KV_W, :] = v[lo:lo + width].T

        if win_first[g] == 0:
            write_rows()
        else:
            pl.when(t >= win_first[g])(write_rows)


def _ab_in(x, nw, w_in, cos, sin, pool_w, pool_scale, layer, halo, *, nb, nt, tm, stride, dils, pos0, tail_rows,
           tail_per_tile, win_shapes, win_blocks, win_index, win_first, win_chunks):
    rows = x.shape[0]
    hist = _hist_rows(POOL_STATE, stride)
    assert halo.shape == (nb, hist, D_POOL)
    assert cos.shape == (nt * tm, V7X_LANES)
    table_index = lambda b, t: (t, 0)
    seq_rows = nt * tm
    tail_total, tail_spec = _tail_spec(tail_rows, D_POOL, nt, tail_per_tile)
    qkv_shapes = [jax.ShapeDtypeStruct((nb, seq_rows // d, d * GROUP_W), BF16) for d in dils]
    qkv_specs = [pl.BlockSpec((None, tm // d, d * GROUP_W), lambda b, t: (b, t, 0)) for d in dils]
    vm = (2 * _nbytes((tm, D_MODEL), F32) + _nbytes((D_MODEL, QKV_W + D_POOL), BF16) + 2 * _nbytes((tm, QKV_W), BF16)
          + 2 * sum(_nbytes(blk, F32) for blk in win_blocks) + 4 * _nbytes((tm, D_POOL), F32)
          + _nbytes((2 * hist + tm, D_POOL), F32) + 2 * _nbytes((tail_rows, D_POOL), F32) + 8 * _nbytes((tm, A_WIDTH), F32))
    in_specs = [pl.BlockSpec((tm, D_MODEL), lambda b, t: (b * nt + t, 0)),
                _layer_block((1, D_MODEL), layer), _layer_block((D_MODEL, QKV_W + D_POOL), layer),
                pl.BlockSpec((tm, V7X_LANES), table_index), pl.BlockSpec((tm, V7X_LANES), table_index),
                _layer_block((len(POOL_WINDOWS), POOL_GC, POOL_GC), layer), _layer_block((1, D_POOL), layer),
                pl.BlockSpec((None, hist, D_POOL), lambda b, t: (b, 0, 0))]
    args = [x, nw, w_in, cos, sin, pool_w, pool_scale, halo]
    return pl.pallas_call(
        functools.partial(_ab_in_kernel, tm=tm, stride=stride, dils=dils, pos0=pos0, hist=hist, tail_rows=tail_rows,
                          win_first=win_first, win_chunks=win_chunks),
        grid=(nb, nt),
        in_specs=in_specs,
        out_specs=qkv_specs + [pl.BlockSpec((tm, D_POOL), lambda b, t: (b * nt + t, 0))]
                  + [pl.BlockSpec(blk, idx) for blk, idx in zip(win_blocks, win_index)] + [tail_spec],
        out_shape=qkv_shapes + [jax.ShapeDtypeStruct((rows, D_POOL), BF16)]
                  + [jax.ShapeDtypeStruct(s, F32) for s in win_shapes]
                  + [jax.ShapeDtypeStruct((nb, tail_total, D_POOL), F32)],
        scratch_shapes=[pltpu.VMEM((hist + tm, D_POOL), F32), pltpu.VMEM((LANE_CHUNKS, tm, V7X_LANES), F32)],
        compiler_params=_cparams(2, vm),
        name="ab_in",
    )(*args)


def _attn_kernel(*refs, side_seg):
    if side_seg:
        q_ref, kc_ref, kp_ref, vc_ref, vp_ref, sq_ref, c0_ref, c1_ref, c2_ref, out_ref, lse_ref, sout_ref = refs
        _attn_s_body(sq_ref, (c0_ref, c1_ref, c2_ref), sout_ref, side_seg)
    else:
        q_ref, kc_ref, kp_ref, vc_ref, vp_ref, out_ref, lse_ref = refs
    n = pl.program_id(2)
    blk = ATT_BLK
    qi = lax.broadcasted_iota(jnp.int32, (blk, 2 * blk), 0)
    kj = lax.broadcasted_iota(jnp.int32, (blk, 2 * blk), 1)
    band = (kj >= qi) & (kj <= qi + SPAN)
    band_first = band & ((kj >= blk) | (n > 0))
    lane = lax.broadcasted_iota(jnp.int32, (blk, V7X_LANES), 1)
    even_head = lane < HEAD_DIM
    lane_head = lane // LSE_LANES

    for c in range(ATT_TQ // blk):
        rows = slice(c * blk, (c + 1) * blk)
        mask = band_first if c == 0 else band
        mask2 = jnp.concatenate([mask, mask], axis=0)
        lse_blk = jnp.zeros((blk, V7X_LANES), F32)
        for hp in range(A_HEADS // 2):
            ls = slice(hp * V7X_LANES, (hp + 1) * V7X_LANES)
            q2 = q_ref[rows, ls].astype(F32)
            qs = jnp.concatenate([jnp.where(even_head, q2, 0.0), jnp.where(even_head, 0.0, q2)], axis=0).astype(BF16)
            if c == 0:
                k2 = jnp.concatenate([kp_ref[:, ls], kc_ref[0:blk, ls]], axis=0)
                v2 = jnp.concatenate([vp_ref[:, ls], vc_ref[0:blk, ls]], axis=0)
            else:
                k2 = kc_ref[(c - 1) * blk:(c + 1) * blk, ls]
                v2 = vc_ref[(c - 1) * blk:(c + 1) * blk, ls]
            s = jnp.where(mask2, _dot_nt(qs, k2), NEG_INF)
            m = jnp.max(s, axis=-1, keepdims=True)
            p = jnp.exp(s - m)
            den = jnp.sum(p, axis=-1, keepdims=True)
            o = _dot(p.astype(BF16), v2) / den
            lse = m + jnp.log(den)
            out_ref[rows, ls] = jnp.where(even_head, o[0:blk], o[blk:2 * blk]).astype(BF16)
            lse_blk = jnp.where(lane_head == 2 * hp, lse[0:blk],
                                jnp.where(lane_head == 2 * hp + 1, lse[blk:2 * blk], lse_blk))
        lse_ref[rows, :] = lse_blk


def _attn_group(qkv_g, g, nb, seq, side=None):
    dil = DIL_PATTERNS[g][1]
    sub = seq // dil
    tq, blk = ATT_TQ, ATT_BLK
    nq = sub // tq
    parts = GROUP_W // A_WIDTH
    cur = lambda part: pl.BlockSpec((None, tq, A_WIDTH), lambda b, r, n: (b, n, r * parts + part))
    prev = lambda part: pl.BlockSpec((None, blk, A_WIDTH),
                                     lambda b, r, n: (b, jnp.maximum(n * (tq // blk) - 1, 0), r * parts + part))
    in_specs = [cur(0), cur(1), prev(1), cur(2), prev(2)]
    args = [qkv_g] * 5
    o_spec = pl.BlockSpec((None, tq, A_WIDTH), lambda b, r, n: (b, n, r))
    l_spec = pl.BlockSpec((None, tq, V7X_LANES), lambda b, r, n: (b, n, r))
    out_specs = [o_spec, l_spec]
    out_shape = [jax.ShapeDtypeStruct((nb, sub, dil * A_WIDTH), BF16),
                 jax.ShapeDtypeStruct((nb, sub, dil * V7X_LANES), F32)]
    vm = 2 * (10 * _nbytes((tq, A_WIDTH), BF16) + 3 * _nbytes((tq, V7X_LANES), F32))
    side_seg = 0
    if side is not None:
        s_qkv, caches_t, jc, first_seq, n_seq, side_seg = side
        steps = nb * dil * nq
        n = n_seq // steps
        assert n * steps == n_seq and first_seq % n == 0 and first_seq + n_seq <= caches_t[0].shape[1]
        step = lambda b, r, i: (b * dil + r) * nq + i
        in_specs += ([pl.BlockSpec((n * side_seg, QKV_W), lambda b, r, i: (first_seq // n + step(b, r, i), 0))]
                     + [pl.BlockSpec((None, n, KV_W, c.shape[3]),
                                     lambda b, r, i: (jc, first_seq // n + step(b, r, i), 0, 0)) for c in caches_t])
        args += [s_qkv, *caches_t]
        out_specs.append(pl.BlockSpec((n * side_seg, A_WIDTH), lambda b, r, i: (step(b, r, i), 0)))
        out_shape.append(jax.ShapeDtypeStruct((n_seq * side_seg, A_WIDTH), F32))
        cache_bytes = n * sum(_nbytes(c.shape[2:], F32) for c in caches_t)
        vm += (2 * (cache_bytes + _nbytes((n * side_seg, QKV_W), F32))
               + 16 * _nbytes((A_HEADS * side_seg, ATT_S_CHUNK), F32) + 4 * _nbytes((A_WIDTH, ATT_S_CHUNK), F32))
    res = pl.pallas_call(
        functools.partial(_attn_kernel, side_seg=side_seg),
        grid=(nb, dil, nq),
        in_specs=in_specs,
        out_specs=out_specs,
        out_shape=out_shape,
        compiler_params=_cparams(3, vm),
        name=f"attn_d{dil}",
    )(*args)
    return tuple(res)


def _attn_s_body(qkv_ref, c_refs, out_ref, seg):
    hq = A_HEADS * seg
    row = lax.broadcasted_iota(jnp.int32, (hq, A_WIDTH), 0)
    lane = lax.broadcasted_iota(jnp.int32, (hq, A_WIDTH), 1)
    head_lanes = (lane // HEAD_DIM) == (row // seg)
    out_lane = lax.broadcasted_iota(jnp.int32, (seg, A_WIDTH), 1) // HEAD_DIM
    pad = V7X_LANES - seg
    for s in range(c_refs[0].shape[0]):
        new = qkv_ref[s * seg:(s + 1) * seg, :]
        scores, values = [], []
        for g, ((win, dil), c_ref) in enumerate(zip(DIL_PATTERNS, c_refs)):
            base = g * GROUP_W
            q = new[:, base:base + A_WIDTH]
            qrows = jnp.where(head_lanes, jnp.concatenate([q] * A_HEADS, axis=0), 0.0).astype(BF16)
            nk = c_ref.shape[2]
            ch = min(nk, ATT_S_CHUNK)
            tok = lax.broadcasted_iota(jnp.int32, (hq, ch), 0) % seg
            key = lax.broadcasted_iota(jnp.int32, (hq, ch), 1)
            same_class = (key % dil) == (tok % dil)
            for lo in range(0, nk, ch):
                valid = same_class & (key >= tok) if lo < seg else same_class
                kt = c_ref[s, 0:A_WIDTH, lo:lo + ch].astype(BF16)
                scores.append(jnp.where(valid, _dot(qrows, kt), NEG_INF))
                values.append((c_ref, lo, ch))
            kn = jnp.concatenate([new[:, base + A_WIDTH:base + 2 * A_WIDTH], jnp.zeros((pad, A_WIDTH), F32)], axis=0)
            vn = jnp.concatenate([new[:, base + 2 * A_WIDTH:base + 3 * A_WIDTH], jnp.zeros((pad, A_WIDTH), F32)], axis=0)
            tok_n = lax.broadcasted_iota(jnp.int32, (hq, V7X_LANES), 0) % seg
            key_n = lax.broadcasted_iota(jnp.int32, (hq, V7X_LANES), 1)
            valid_n = (key_n <= tok_n) & (((tok_n - key_n) % dil) == 0)
            scores.append(jnp.where(valid_n, _dot_nt(qrows, kn.astype(BF16)), NEG_INF))
            values.append(vn.astype(BF16))
        m = scores[0].max(axis=-1, keepdims=True)
        for sc in scores[1:]:
            m = jnp.maximum(m, sc.max(axis=-1, keepdims=True))
        den = jnp.zeros((hq, 1), F32)
        acc = jnp.zeros((hq, A_WIDTH), F32)
        for sc, val in zip(scores, values):
            p = jnp.exp(sc - m)
            den = den + p.sum(axis=-1, keepdims=True)
            if isinstance(val, tuple):
                c_ref, lo, ch = val
                acc = acc + _dot_nt(p.astype(BF16), c_ref[s, A_WIDTH:KV_W, lo:lo + ch].astype(BF16))
            else:
                acc = acc + _dot(p.astype(BF16), val)
        acc = acc / den
        o = jnp.zeros((seg, A_WIDTH), F32)
        for hh in range(A_HEADS):
            o = jnp.where(out_lane == hh, acc[hh * seg:(hh + 1) * seg], o)
        out_ref[s * seg:(s + 1) * seg, :] = o


def _rope_tables(pos):
    half = HEAD_DIM // 2
    inv = jnp.power(ROPE_THETA, -jnp.arange(half, dtype=F32) / half)
    reps = V7X_LANES // half
    ang = pos.astype(F32)[:, None] * jnp.tile(inv, reps)[None, :]
    sign = jnp.tile(jnp.concatenate([-jnp.ones((half,), F32), jnp.ones((half,), F32)]), reps // 2)
    return jnp.cos(ang), jnp.sin(ang) * sign[None, :]


def _prompt_mixers(w, nb, seq):
    tm = MIXER_ROW_TILE
    nt = seq // tm
    cos, sin = _rope_tables(jnp.arange(seq, dtype=jnp.int32))
    dils = tuple(d for _, d in DIL_PATTERNS)
    win_tot = tuple(min(win, seq) for win, _ in DIL_PATTERNS)
    win_r = tuple(min(wt, tm) for wt in win_tot)
    win_first = tuple(nt - max(wt // tm, 1) for wt in win_tot)
    win_cfg = dict(
        win_shapes=[(nb, KV_W, wt) for wt in win_tot],
        win_blocks=[(1, KV_W, r) for r in win_r],
        win_index=[(lambda b, t, first=first: (b, 0, jnp.maximum(t - first, 0))) for first in win_first],
        win_first=win_first,
        win_chunks=tuple(((0, tm - r, r),) for r in win_r))
    pool_halo = jnp.zeros((nb, _hist_rows(POOL_STATE, 1), D_POOL), F32)
    conv_halo = jnp.zeros((nb, _hist_rows(CONV_W - 1, 1), D_MODEL), F32)

    def ab_mixer(x, j, sample):
        q0, q1, q2, p, w0, w1, w2, ptail = _ab_in_call(
            x, w, j, cos, sin, pool_halo, nb=nb, nt=nt, tm=tm, stride=1, dils=dils, pos0=0,
            tail_rows=pool_halo.shape[1], tail_per_tile=False, **win_cfg)
        s_qkv, caches_t, nseq, seg = sample
        half = nseq // 2
        o2, l2, att_lo = _attn_group(q2, 2, nb, seq, side=(s_qkv, caches_t, j, 0, half, seg))
        o1, l1, att_hi = _attn_group(q1, 1, nb, seq, side=(s_qkv, caches_t, j, half, nseq - half, seg))
        o0, l0 = _attn_group(q0, 0, nb, seq)
        groups = [(o0, l0, dils[0]), (o1, l1, dils[1]), (o2, l2, dils[2])]
        return (groups, p, (w0, w1, w2), ptail[:, ptail.shape[1] - POOL_STATE:],
                jnp.concatenate([att_lo, att_hi], axis=0))

    def conv_mixer(x, j):
        x, ctail = _conv_layer(x, w["mix_norm_c"], w["conv_w_in"], w["conv_w"], w["conv_w_out"], j, conv_halo,
                               nb=nb, nt=nt, tm=tm, stride=1, tail_rows=conv_halo.shape[1], tail_per_tile=False)
        return x, ctail[:, ctail.shape[1] - (CONV_W - 1):]

    def present_windows(wins):
        out = []
        for g in range(N_DIL):
            wt = jnp.stack(wins[g], axis=0).reshape(len(wins[g]), nb, 2, A_HEADS, HEAD_DIM, win_tot[g])
            out.append(jnp.transpose(wt, (0, 1, 5, 2, 3, 4)))
        return out

    return ab_mixer, conv_mixer, present_windows


def _ab_in_call(x, w, j, cos, sin, halo, **kw):
    return _ab_in(x, w["mix_norm_ab"], w["ab_w_in"], cos, sin, w["pool_w"], w["pool_scale"], j, halo, **kw)


def _sample_mixers(w, nseq, seg, pool_state, conv_state):
    tm = SAMPLE_ROW_TILE
    rows = seg * nseq
    nt = rows // tm
    tok_tile = tm // nseq
    cos, sin = _rope_tables(PAST_LEN + jnp.arange(seg, dtype=jnp.int32))
    cos, sin = jnp.repeat(cos, nseq, axis=0), jnp.repeat(sin, nseq, axis=0)
    win_cfg = dict(
        win_shapes=[(seg, KV_W, nseq)] * N_DIL,
        win_blocks=[(tok_tile, KV_W, nseq)] * N_DIL,
        win_index=[lambda b, t: (t, 0, 0)] * N_DIL,
        win_first=(0,) * N_DIL,
        win_chunks=(tuple((i, i * nseq, nseq) for i in range(tok_tile)),) * N_DIL)

    def ab_in(x, j):
        state = jnp.transpose(pool_state[j], (1, 0, 2))
        halo = state.reshape(1, POOL_STATE * nseq, D_POOL)
        q0, q1, q2, p, w0, w1, w2, u = _ab_in_call(
            x, w, j, cos, sin, halo, nb=1, nt=nt, tm=tm, stride=nseq, dils=(1,) * N_DIL, pos0=PAST_LEN,
            tail_rows=tm, tail_per_tile=True, **win_cfg)
        qkv = jnp.concatenate([q.reshape(seg, nseq, GROUP_W) for q in (q0, q1, q2)], axis=-1)
        qkv = jnp.transpose(qkv, (1, 0, 2)).reshape(rows, QKV_W).astype(F32)
        u = u.reshape(seg, nseq, D_POOL)
        pool = jnp.transpose(jnp.concatenate([state, u], axis=0)[-POOL_STATE:], (1, 0, 2))
        return qkv, p, (w0, w1, w2), pool

    def attn_rows(a):
        return jnp.transpose(a.reshape(nseq, seg, A_WIDTH), (1, 0, 2)).reshape(rows, A_WIDTH).astype(BF16)

    def conv_mixer(x, j):
        state = jnp.transpose(conv_state[j], (1, 0, 2))
        halo = state.reshape(1, (CONV_W - 1) * nseq, D_MODEL)
        x, cu = _conv_layer(x, w["mix_norm_c"], w["conv_w_in"], w["conv_w"], w["conv_w_out"], j, halo,
                            nb=1, nt=nt, tm=tm, stride=nseq, tail_rows=tm, tail_per_tile=True)
        cu = cu.reshape(seg, nseq, D_MODEL)
        return x, jnp.transpose(jnp.concatenate([state, cu], axis=0)[-(CONV_W - 1):], (1, 0, 2))

    def present_windows(wins):
        out = []
        for g in range(N_DIL):
            wt = jnp.stack(wins[g], axis=0).reshape(len(wins[g]), seg, 2, A_HEADS, HEAD_DIM, nseq)
            out.append(jnp.transpose(wt, (0, 5, 1, 2, 3, 4)))
        return out

    return ab_in, attn_rows, conv_mixer, present_windows


def _trunks(xp, xs, w, caches, pool_state, conv_state, *, nb, seq, nseq, seg):
    depth = w["ffn1_norm"].shape[0]
    p_ab, p_conv, p_windows = _prompt_mixers(w, nb, seq)
    s_ab_in, s_attn_rows, s_conv, s_windows = _sample_mixers(w, nseq, seg, pool_state, conv_state)
    caches_t = [jnp.transpose(c, (0, 1, 3, 4, 5, 2)).reshape(c.shape[0], nseq, KV_W, c.shape[2]) for c in caches]
    ffn1 = lambda x, l, **kw: _ffn(x, w["ffn1_norm"], w["ffn1_w_gu"], w["ffn1_w_down"], l, **kw)
    ffn2 = lambda x, l, **kw: _ffn(x, w["ffn2_norm"], w["ffn2_w_gu"], w["ffn2_w_down"], l,
                                   final_w=w["final_norm"] if l == depth - 1 else None, **kw)
    wins_p, wins_s = [[] for _ in range(N_DIL)], [[] for _ in range(N_DIL)]
    pools_p, pools_s, convs_p, convs_s = [], [], [], []
    for l in range(depth):
        j = l // 2
        if l % 2 == 0:
            xs = ffn1(xs, l)
            qkv_s, p_s, win_s, pool_s = s_ab_in(xs, j)
            xp = ffn1(xp, l)
            a_p, p_p, win_p, pool_p, att_s = p_ab(xp, j, (qkv_s, caches_t, nseq, seg))
            xp = ffn2(xp, l, premix=(a_p, p_p, w["ab_w_out"], j))
            xs = ffn2(xs, l, premix=(s_attn_rows(att_s), p_s, w["ab_w_out"], j))
            for g in range(N_DIL):
                wins_p[g].append(win_p[g])
                wins_s[g].append(win_s[g])
            pools_p.append(pool_p)
            pools_s.append(pool_s)
        else:
            xp, conv_p = p_conv(ffn1(xp, l), j)
            xs, conv_s = s_conv(ffn1(xs, l), j)
            xp, xs = ffn2(xp, l), ffn2(xs, l)
            convs_p.append(conv_p)
            convs_s.append(conv_s)
    return ((xp, p_windows(wins_p), jnp.stack(pools_p, axis=0), jnp.stack(convs_p, axis=0)),
            (xs, s_windows(wins_s), jnp.stack(pools_s, axis=0), jnp.stack(convs_s, axis=0)))


def kernel(x_prompt, x_sample, cache_win0, cache_win1, cache_win2, state_pool, state_conv, ffn1_norm, ffn1_w_gu, ffn1_w_down, mix_norm, ffn2_norm, ffn2_w_gu, ffn2_w_down, ab_w_in, ab_w_out, pool_w, pool_scale, conv_w_in, conv_w, conv_w_out, final_norm):
    batch, seq, _ = x_prompt.shape
    dbatch, dseq, _ = x_sample.shape
    assert seq % ROW_TILE == 0 and (dbatch * dseq) % ROW_TILE == 0 and dbatch % 2 == 0
    assert SAMPLE_ROW_TILE % dbatch == 0 and (dbatch * dseq) % SAMPLE_ROW_TILE == 0 and dbatch % V7X_SUBLANES == 0
    assert all((seq // dil) % ATT_TQ == 0 and ROW_TILE % dil == 0 for _, dil in DIL_PATTERNS)
    assert dseq == V7X_SUBLANES and PAST_LEN + 1 >= max(POOL_WINDOWS) and PAST_LEN >= POOL_STATE
    assert all(c.shape[2] == win and win == dil * SPAN for c, (win, dil) in zip((cache_win0, cache_win1, cache_win2), DIL_PATTERNS))
    row3 = lambda a: a.reshape(a.shape[0], 1, a.shape[1])
    w = dict(
        ffn1_norm=row3(ffn1_norm), ffn1_w_gu=ffn1_w_gu.astype(BF16), ffn1_w_down=ffn1_w_down.astype(BF16),
        ffn2_norm=row3(ffn2_norm), ffn2_w_gu=ffn2_w_gu.astype(BF16), ffn2_w_down=ffn2_w_down.astype(BF16),
        mix_norm_ab=row3(mix_norm[0::2]), mix_norm_c=row3(mix_norm[1::2]),
        ab_w_in=ab_w_in.astype(BF16), ab_w_out=ab_w_out.astype(BF16), pool_w=pool_w.astype(BF16), pool_scale=row3(pool_scale),
        conv_w_in=conv_w_in.astype(BF16), conv_w=conv_w, conv_w_out=conv_w_out.astype(BF16), final_norm=final_norm)
    xs = jnp.transpose(x_sample, (1, 0, 2)).reshape(dseq * dbatch, D_MODEL)
    (y_p, win_p, pool_p, conv_p), (y_s, win_s, pool_s, conv_s) = _trunks(
        x_prompt.reshape(batch * seq, D_MODEL), xs, w, (cache_win0, cache_win1, cache_win2), state_pool, state_conv,
        nb=batch, seq=seq, nseq=dbatch, seg=dseq)
    y_s = jnp.transpose(y_s.reshape(dseq, dbatch, D_MODEL), (1, 0, 2))
    return (y_p.reshape(batch, seq, D_MODEL), y_s,
            win_p[0], win_p[1], win_p[2], pool_p, conv_p,
            win_s[0], win_s[1], win_s[2], pool_s, conv_s)
```

```python
import functools

import jax
import jax.numpy as jnp
from jax import lax
from jax.experimental import pallas as pl
from jax.experimental.pallas import tpu as pltpu

F32 = jnp.float32
BF16 = jnp.bfloat16

D_MODEL = 1024
HEAD_DIM = 64
A_HEADS = 8
A_WIDTH = A_HEADS * HEAD_DIM
DIL_PATTERNS = ((128, 1), (512, 4), (2048, 16))
N_DIL = len(DIL_PATTERNS)
SPAN = 128
GROUP_W = 3 * A_WIDTH
QKV_W = N_DIL * GROUP_W
KV_W = 2 * A_WIDTH
POOL_WINDOWS = (2, 4, 8, 16)
D_POOL = 512
POOL_GC = 128
POOL_STATE = 15
CONV_W = 3
D_FF = 2816
ROPE_THETA = 10000.0
EPS = 1e-6
PAST_LEN = 2048

V7X_LANES = 128
V7X_SUBLANES = 8
V7X_MXU_DIM = 256
V7X_VMEM_USABLE = 56 * 1024 * 1024
MIB = 1024 * 1024

ROW_TILE = 512
MIXER_ROW_TILE = 512
SAMPLE_ROW_TILE = 256
ATT_S_CHUNK = 512
ATT_TQ = 256
ATT_BLK = SPAN
LSE_LANES = V7X_LANES // A_HEADS
LANE_CHUNKS = A_WIDTH // V7X_LANES
NEG_INF = float("-inf")


def _round_up(n, m):
    return -(-n // m) * m


def _nbytes(shape, dtype):
    n = 1
    for s in shape:
        n *= s
    return n * jnp.dtype(dtype).itemsize


def _cparams(n_axes, block_bytes):
    limit = int(min(V7X_VMEM_USABLE, block_bytes * 1.25 + 12 * MIB))
    return pltpu.CompilerParams(dimension_semantics=("arbitrary",) * n_axes, vmem_limit_bytes=limit)


def _layer_block(shape, layer):
    nd = len(shape)
    return pl.BlockSpec((None,) + tuple(shape), lambda *_: (layer,) + (0,) * nd, pipeline_mode=pl.Buffered(1))


def _const_block(shape):
    nd = len(shape)
    return pl.BlockSpec(tuple(shape), lambda *_: (0,) * nd, pipeline_mode=pl.Buffered(1))


def _rms(x, w):
    return x * lax.rsqrt(jnp.mean(x * x, axis=-1, keepdims=True) + EPS) * w


def _dot(a, b):
    return jnp.dot(a, b, preferred_element_type=F32)


def _dot_nt(a, b):
    return lax.dot_general(a, b, (((1,), (1,)), ((), ())), preferred_element_type=F32)


def _merge_groups(group_refs, dils, a_scr, regroup, tm):
    blk = ATT_BLK
    lane = lax.broadcasted_iota(jnp.int32, (blk, V7X_LANES), 1)
    even_head = lane < HEAD_DIM
    for (o_ref, l_ref), d, (o_n, l_n) in zip(group_refs, dils, regroup):
        for r in range(d):
            dst = pl.ds(r, tm // d, stride=d) if d > 1 else slice(0, tm)
            for hp in range(LANE_CHUNKS):
                lo = r * A_WIDTH + hp * V7X_LANES
                o_n[hp, dst, :] = o_ref[:, lo:lo + V7X_LANES].astype(F32)
            l_n[dst, :] = l_ref[:, r * V7X_LANES:(r + 1) * V7X_LANES]
    for c in range(tm // blk):
        rows = slice(c * blk, (c + 1) * blk)
        lses = [l_n[rows, :] for _, l_n in regroup]
        mx = functools.reduce(jnp.maximum, lses)
        es = [jnp.exp(l - mx) for l in lses]
        tot = functools.reduce(jnp.add, es)
        ws = [e / tot for e in es]
        for hp in range(LANE_CHUNKS):
            ce, co = 2 * hp * LSE_LANES, (2 * hp + 1) * LSE_LANES
            acc = None
            for w, (o_n, _) in zip(ws, regroup):
                term = jnp.where(even_head, w[:, ce:ce + 1], w[:, co:co + 1]) * o_n[hp, rows, :]
                acc = term if acc is None else acc + term
            a_scr[rows, hp * V7X_LANES:(hp + 1) * V7X_LANES] = acc.astype(BF16)


def _ffn_kernel(*refs, premix, final, merge_dils, tm):
    refs = list(refs)
    x_ref, nw_ref, wgu_ref, wd_ref = refs[:4]
    del refs[:4]
    x = x_ref[...]
    if premix == "rows":
        a_ref, p_ref, wo_ref = refs[:3]
        del refs[:3]
        x = x + _dot(a_ref[...], wo_ref[0:A_WIDTH, :]) + _dot(p_ref[...], wo_ref[A_WIDTH:A_WIDTH + D_POOL, :])
    elif premix == "merge":
        ng = len(merge_dils)
        cur_refs = [(refs[2 * g], refs[2 * g + 1]) for g in range(ng)]
        next_refs = [(refs[2 * ng + 2 * g], refs[2 * ng + 2 * g + 1]) for g in range(ng)]
        p_ref, wo_ref = refs[4 * ng:4 * ng + 2]
        del refs[:4 * ng + 2]
        mix_scr = refs[-1 - 2 * ng]
        regroup = [(refs[-2 * ng + 2 * g], refs[-2 * ng + 2 * g + 1]) for g in range(ng)]
        del refs[-1 - 2 * ng:]
        pl.when(pl.program_id(0) == 0)(lambda: _merge_groups(cur_refs, merge_dils, mix_scr, regroup, tm))
        x = x + _dot(mix_scr[...], wo_ref[0:A_WIDTH, :]) + _dot(p_ref[...], wo_ref[A_WIDTH:A_WIDTH + D_POOL, :])
    fw_ref = refs.pop(0) if final else None
    o_ref, a_scr = refs
    o_ref[...] = _ffn_tile(x, nw_ref, wgu_ref, wd_ref, a_scr, fw_ref)
    if premix == "merge":
        _merge_groups(next_refs, merge_dils, mix_scr, regroup, tm)


def _ffn_tile(x, nw_ref, wgu_ref, wd_ref, a_scr, fw_ref):
    h = _rms(x, nw_ref[...]).astype(BF16)
    for lo in range(0, D_FF, V7X_MXU_DIM):
        g = _dot(h, wgu_ref[:, lo:lo + V7X_MXU_DIM])
        u = _dot(h, wgu_ref[:, D_FF + lo:D_FF + lo + V7X_MXU_DIM])
        a_scr[:, lo:lo + V7X_MXU_DIM] = (g * jax.nn.sigmoid(g) * u).astype(BF16)
    y = x + 0.5 * _dot(a_scr[...], wd_ref[...])
    return y if fw_ref is None else _rms(y, fw_ref[...])


def _ffn_pair_kernel(xa_ref, xb_ref, nw_ref, wgu_ref, wd_ref, *rest, final, steps_a):
    fw_ref = rest[0] if final else None
    oa_ref, ob_ref, a_scr = rest[-3:]
    i = pl.program_id(0)

    @pl.when(i < steps_a)
    def _():
        oa_ref[...] = _ffn_tile(xa_ref[...], nw_ref, wgu_ref, wd_ref, a_scr, fw_ref)

    @pl.when(i >= steps_a)
    def _():
        ob_ref[...] = _ffn_tile(xb_ref[...], nw_ref, wgu_ref, wd_ref, a_scr, fw_ref)


def _ffn_pair(xa, xb, nw, wgu, wd, layer, *, final_w=None):
    tm = ROW_TILE
    steps_a, steps_b = xa.shape[0] // tm, xb.shape[0] // tm
    final = final_w is not None
    a_index = lambda i: (jnp.minimum(i, steps_a - 1), 0)
    b_index = lambda i: (jnp.maximum(i - steps_a, 0), 0)
    in_specs = [pl.BlockSpec((tm, D_MODEL), a_index), pl.BlockSpec((tm, D_MODEL), b_index),
                _layer_block((1, D_MODEL), layer), _layer_block((D_MODEL, 2 * D_FF), layer),
                _layer_block((D_FF, D_MODEL), layer)]
    args = [xa, xb, nw, wgu, wd]
    if final:
        in_specs.append(_const_block((1, D_MODEL)))
        args.append(final_w.reshape(1, D_MODEL))
    vm = (8 * _nbytes((tm, D_MODEL), F32) + _nbytes((D_MODEL, 2 * D_FF), BF16) + _nbytes((D_FF, D_MODEL), BF16)
          + _nbytes((tm, D_FF), BF16))
    return pl.pallas_call(
        functools.partial(_ffn_pair_kernel, final=final, steps_a=steps_a),
        grid=(steps_a + steps_b,),
        in_specs=in_specs,
        out_specs=[pl.BlockSpec((tm, D_MODEL), a_index), pl.BlockSpec((tm, D_MODEL), b_index)],
        out_shape=[jax.ShapeDtypeStruct(xa.shape, F32), jax.ShapeDtypeStruct(xb.shape, F32)],
        scratch_shapes=[pltpu.VMEM((tm, D_FF), BF16)],
        compiler_params=_cparams(1, vm),
        name="ffn_pair" + ("_final" if final else ""),
    )(*args)


def _ffn(x, nw, wgu, wd, layer, *, final_w=None, premix=None):
    rows = x.shape[0]
    tm = ROW_TILE
    final = final_w is not None
    in_specs = [pl.BlockSpec((tm, D_MODEL), lambda i: (i, 0)),
                _layer_block((1, D_MODEL), layer), _layer_block((D_MODEL, 2 * D_FF), layer),
                _layer_block((D_FF, D_MODEL), layer)]
    args = [x, nw, wgu, wd]
    vm = (4 * _nbytes((tm, D_MODEL), F32) + _nbytes((D_MODEL, 2 * D_FF), BF16) + _nbytes((D_FF, D_MODEL), BF16)
          + _nbytes((tm, D_FF), BF16))
    scratch = [pltpu.VMEM((tm, D_FF), BF16)]
    mode, merge_dils = None, ()
    if premix is not None:
        a, p, w_out, j = premix
        if isinstance(a, (list, tuple)):
            mode, merge_dils = "merge", tuple(d for _, _, d in a)
            scratch.append(pltpu.VMEM((tm, A_WIDTH), BF16))
            last = rows // tm - 1
            for index in (lambda i: (i, 0), lambda i: (jnp.minimum(i + 1, last), 0)):
                for o, lse, d in a:
                    assert tm % (d * 2 * V7X_SUBLANES) == 0
                    in_specs += [pl.BlockSpec((tm // d, d * A_WIDTH), index),
                                 pl.BlockSpec((tm // d, d * V7X_LANES), index)]
                    args += [o.reshape(rows // d, d * A_WIDTH), lse.reshape(rows // d, d * V7X_LANES)]
                    vm += 2 * _nbytes((tm, A_WIDTH), BF16) + 2 * _nbytes((tm, V7X_LANES), F32)
            for _ in a:
                scratch += [pltpu.VMEM((LANE_CHUNKS, tm, V7X_LANES), F32), pltpu.VMEM((tm, V7X_LANES), F32)]
                vm += _nbytes((tm, A_WIDTH + V7X_LANES), F32)
        else:
            mode = "rows"
            in_specs.append(pl.BlockSpec((tm, A_WIDTH), lambda i: (i, 0)))
            args.append(a)
        in_specs += [pl.BlockSpec((tm, D_POOL), lambda i: (i, 0)), _layer_block((A_WIDTH + D_POOL, D_MODEL), j)]
        args += [p, w_out]
        vm += 4 * _nbytes((tm, A_WIDTH), BF16) + _nbytes((D_MODEL, D_MODEL), BF16)
    if final:
        in_specs.append(_const_block((1, D_MODEL)))
        args.append(final_w.reshape(1, D_MODEL))
    return pl.pallas_call(
        functools.partial(_ffn_kernel, premix=mode, final=final, merge_dils=merge_dils, tm=tm),
        grid=(rows // tm,),
        in_specs=in_specs,
        out_specs=pl.BlockSpec((tm, D_MODEL), lambda i: (i, 0)),
        out_shape=jax.ShapeDtypeStruct((rows, D_MODEL), F32),
        scratch_shapes=scratch,
        compiler_params=_cparams(1, vm),
        name="ffn" + {None: "", "rows": "_mix", "merge": "_merge"}[mode] + ("_final" if final else ""),
    )(*args)


def _hist_rows(positions, stride):
    return _round_up(positions * stride, V7X_SUBLANES)


def _init_history(ext, halo_ref, hist):
    @pl.when(pl.program_id(1) == 0)
    def _():
        ext[0:hist, :] = halo_ref[...]


def _carry_history(ext, hist, tm):
    ext[0:hist, :] = ext[tm:tm + hist, :]


def _conv_kernel(x_ref, nw_ref, win_ref, cw_ref, wout_ref, halo_ref, o_ref, tail_ref, ext, *, tm, stride, hist, tail_rows):
    _init_history(ext, halo_ref, hist)
    x = x_ref[...]
    h = _rms(x, nw_ref[...]).astype(BF16)
    gb = _dot(h, win_ref[:, 0:D_MODEL])
    gc = _dot(h, win_ref[:, D_MODEL:2 * D_MODEL])
    v = _dot(h, win_ref[:, 2 * D_MODEL:3 * D_MODEL])
    ext[hist:hist + tm, :] = gc * v
    cw = cw_ref[...]
    y = cw[CONV_W - 1] * ext[hist:hist + tm, :]
    for back in range(1, CONV_W):
        lo = hist - back * stride
        y = y + cw[CONV_W - 1 - back] * ext[lo:lo + tm, :]
    o_ref[...] = x + _dot((gb * y).astype(BF16), wout_ref[...])
    tail_ref[...] = ext[hist + tm - tail_rows:hist + tm, :]
    _carry_history(ext, hist, tm)


def _tail_spec(tail_rows, width, nt, per_tile):
    if per_tile:
        return nt * tail_rows, pl.BlockSpec((None, tail_rows, width), lambda b, t: (b, t, 0))
    return tail_rows, pl.BlockSpec((None, tail_rows, width), lambda b, t: (b, 0, 0))


def _conv_layer(x, nw, w_in, cw, w_out, layer, halo, *, nb, nt, tm, stride, tail_rows, tail_per_tile):
    hist = _hist_rows(CONV_W - 1, stride)
    assert halo.shape == (nb, hist, D_MODEL)
    rows = x.shape[0]
    tail_total, tail_spec = _tail_spec(tail_rows, D_MODEL, nt, tail_per_tile)
    vm = (8 * _nbytes((tm, D_MODEL), F32) + _nbytes((D_MODEL, 4 * D_MODEL), BF16)
          + _nbytes((hist + tm, D_MODEL), F32) + 4 * _nbytes((hist + tail_rows, D_MODEL), F32))
    return pl.pallas_call(
        functools.partial(_conv_kernel, tm=tm, stride=stride, hist=hist, tail_rows=tail_rows),
        grid=(nb, nt),
        in_specs=[pl.BlockSpec((tm, D_MODEL), lambda b, t: (b * nt + t, 0)),
                  _layer_block((1, D_MODEL), layer), _layer_block((D_MODEL, 3 * D_MODEL), layer),
                  _layer_block((CONV_W, D_MODEL), layer), _layer_block((D_MODEL, D_MODEL), layer),
                  pl.BlockSpec((None, hist, D_MODEL), lambda b, t: (b, 0, 0))],
        out_specs=[pl.BlockSpec((tm, D_MODEL), lambda b, t: (b * nt + t, 0)), tail_spec],
        out_shape=[jax.ShapeDtypeStruct((rows, D_MODEL), F32),
                   jax.ShapeDtypeStruct((nb, tail_total, D_MODEL), F32)],
        scratch_shapes=[pltpu.VMEM((hist + tm, D_MODEL), F32)],
        compiler_params=_cparams(2, vm),
        name="conv_layer",
    )(x, nw, w_in, cw, w_out, halo)


def _ab_in_kernel(x_ref, nw_ref, w_ref, cos_ref, sin_ref, pw_ref, ps_ref, halo_ref,
                  q0_ref, q1_ref, q2_ref, p_ref, win0_ref, win1_ref, win2_ref, tail_ref, ext, perm,
                  *, tm, stride, dils, pos0, hist, tail_rows, win_first, win_chunks):
    qkv_refs = (q0_ref, q1_ref, q2_ref)
    win_refs = (win0_ref, win1_ref, win2_ref)
    t = pl.program_id(1)
    _init_history(ext, halo_ref, hist)
    h = _rms(x_ref[...], nw_ref[...]).astype(BF16)

    cos, sin = cos_ref[...], sin_ref[...]
    lane = lax.broadcasted_iota(jnp.int32, (tm, V7X_LANES), 1)
    first_half = (lane % HEAD_DIM) < (HEAD_DIM // 2)

    def rope(z):
        out = []
        for c in range(z.shape[1] // V7X_LANES):
            zc = z[:, c * V7X_LANES:(c + 1) * V7X_LANES]
            partner = jnp.where(first_half, pltpu.roll(zc, V7X_LANES - HEAD_DIM // 2, 1), pltpu.roll(zc, HEAD_DIM // 2, 1))
            out.append(zc * cos + partner * sin)
        return jnp.concatenate(out, axis=1)

    piece = V7X_MXU_DIM
    slabs = piece // V7X_LANES
    kv = []
    for g in range(N_DIL):
        base, d = g * GROUP_W, dils[g]

        def emit(col, z, g=g, d=d):
            out = qkv_refs[g]
            if d == 1:
                out[:, col:col + piece] = z.astype(BF16)
                return
            for c in range(slabs):
                perm[c] = z[:, c * V7X_LANES:(c + 1) * V7X_LANES]
            for r in range(d):
                rows_r = [perm[c, pl.ds(r, tm // d, stride=d), :] for c in range(slabs)]
                out[:, r * GROUP_W + col:r * GROUP_W + col + piece] = jnp.concatenate(rows_r, axis=1).astype(BF16)

        k_pieces, v_pieces = [], []
        for col in range(0, GROUP_W, piece):
            z = _dot(h, w_ref[:, base + col:base + col + piece])
            if col < A_WIDTH:
                z = rope(z) * (HEAD_DIM ** -0.5)
            elif col < 2 * A_WIDTH:
                z = rope(z)
                k_pieces.append(z)
            else:
                v_pieces.append(z)
            emit(col, z)
        kv.append((k_pieces, v_pieces))

    ext[hist:hist + tm, :] = _dot(h, w_ref[:, QKV_W:QKV_W + D_POOL])
    if pos0 + 1 >= max(POOL_WINDOWS):
        pos = None
    else:
        pos = pos0 + t * (tm // stride) + lax.broadcasted_iota(jnp.int32, (tm, 1), 0) // stride
    for gi, w in enumerate(POOL_WINDOWS):
        sl = slice(gi * POOL_GC, (gi + 1) * POOL_GC)
        tok = ext[hist:hist + tm, sl]
        acc = tok
        for back in range(1, w):
            lo = hist - back * stride
            acc = acc + ext[lo:lo + tm, sl]
        cnt = float(w) if pos is None else jnp.minimum(w, pos + 1).astype(F32)
        d_tok = acc / cnt - tok
        p_ref[:, sl] = (_dot(d_tok.astype(BF16), pw_ref[gi]) * ps_ref[:, sl]).astype(BF16)
    tail_ref[...] = ext[hist + tm - tail_rows:hist + tm, :]
    _carry_history(ext, hist, tm)

    for g, (k_pieces, v_pieces) in enumerate(kv):
        def write_rows(g=g, pieces=k_pieces + v_pieces):
            for dst, lo, width in win_chunks[g]:
                for i, z in enumerate(pieces):
                    win_refs[g][dst, i * piece:(i + 1) * piece, :] = z[lo:lo + width].T

        if win_first[g] == 0:
            write_rows()
        else:
            pl.when(t >= win_first[g])(write_rows)


def _ab_in(x, nw, w_in, cos, sin, pool_w, pool_scale, layer, halo, *, nb, nt, tm, stride, dils, pos0, tail_rows,
           tail_per_tile, win_shapes, win_blocks, win_index, win_first, win_chunks):
    rows = x.shape[0]
    hist = _hist_rows(POOL_STATE, stride)
    assert halo.shape == (nb, hist, D_POOL)
    assert cos.shape == (nt * tm, V7X_LANES)
    table_index = lambda b, t: (t, 0)
    seq_rows = nt * tm
    tail_total, tail_spec = _tail_spec(tail_rows, D_POOL, nt, tail_per_tile)
    qkv_shapes = [jax.ShapeDtypeStruct((nb, seq_rows // d, d * GROUP_W), BF16) for d in dils]
    qkv_specs = [pl.BlockSpec((None, tm // d, d * GROUP_W), lambda b, t: (b, t, 0)) for d in dils]
    vm = (2 * _nbytes((tm, D_MODEL), F32) + _nbytes((D_MODEL, QKV_W + D_POOL), BF16) + 2 * _nbytes((tm, QKV_W), BF16)
          + 2 * sum(_nbytes(blk, F32) for blk in win_blocks) + 4 * _nbytes((tm, D_POOL), F32)
          + _nbytes((2 * hist + tm, D_POOL), F32) + 2 * _nbytes((tail_rows, D_POOL), F32) + 8 * _nbytes((tm, A_WIDTH), F32))
    in_specs = [pl.BlockSpec((tm, D_MODEL), lambda b, t: (b * nt + t, 0)),
                _layer_block((1, D_MODEL), layer), _layer_block((D_MODEL, QKV_W + D_POOL), layer),
                pl.BlockSpec((tm, V7X_LANES), table_index), pl.BlockSpec((tm, V7X_LANES), table_index),
                _layer_block((len(POOL_WINDOWS), POOL_GC, POOL_GC), layer), _layer_block((1, D_POOL), layer),
                pl.BlockSpec((None, hist, D_POOL), lambda b, t: (b, 0, 0))]
    args = [x, nw, w_in, cos, sin, pool_w, pool_scale, halo]
    return pl.pallas_call(
        functools.partial(_ab_in_kernel, tm=tm, stride=stride, dils=dils, pos0=pos0, hist=hist, tail_rows=tail_rows,
                          win_first=win_first, win_chunks=win_chunks),
        grid=(nb, nt),
        in_specs=in_specs,
        out_specs=qkv_specs + [pl.BlockSpec((tm, D_POOL), lambda b, t: (b * nt + t, 0))]
                  + [pl.BlockSpec(blk, idx) for blk, idx in zip(win_blocks, win_index)] + [tail_spec],
        out_shape=qkv_shapes + [jax.ShapeDtypeStruct((rows, D_POOL), BF16)]
                  + [jax.ShapeDtypeStruct(s, F32) for s in win_shapes]
                  + [jax.ShapeDtypeStruct((nb, tail_total, D_POOL), F32)],
        scratch_shapes=[pltpu.VMEM((hist + tm, D_POOL), F32), pltpu.VMEM((LANE_CHUNKS, tm, V7X_LANES), F32)],
        compiler_params=_cparams(2, vm),
        name="ab_in",
    )(*args)


def _attn_kernel(*refs, side_seg):
    if side_seg:
        q_ref, kc_ref, kp_ref, vc_ref, vp_ref, sq_ref, c0_ref, c1_ref, c2_ref, out_ref, lse_ref, sout_ref = refs
        _attn_s_body(sq_ref, (c0_ref, c1_ref, c2_ref), sout_ref, side_seg)
    else:
        q_ref, kc_ref, kp_ref, vc_ref, vp_ref, out_ref, lse_ref = refs
    n = pl.program_id(2)
    blk = ATT_BLK
    qi = lax.broadcasted_iota(jnp.int32, (blk, 2 * blk), 0)
    kj = lax.broadcasted_iota(jnp.int32, (blk, 2 * blk), 1)
    band = (kj >= qi) & (kj <= qi + SPAN)
    band_first = band & ((kj >= blk) | (n > 0))
    lane = lax.broadcasted_iota(jnp.int32, (blk, V7X_LANES), 1)
    even_head = lane < HEAD_DIM
    lane_head = lane // LSE_LANES

    for c in range(ATT_TQ // blk):
        rows = slice(c * blk, (c + 1) * blk)
        mask = band_first if c == 0 else band
        mask2 = jnp.concatenate([mask, mask], axis=0)
        lse_blk = jnp.zeros((blk, V7X_LANES), F32)
        for hp in range(A_HEADS // 2):
            ls = slice(hp * V7X_LANES, (hp + 1) * V7X_LANES)
            q2 = q_ref[rows, ls].astype(F32)
            qs = jnp.concatenate([jnp.where(even_head, q2, 0.0), jnp.where(even_head, 0.0, q2)], axis=0).astype(BF16)
            if c == 0:
                k2 = jnp.concatenate([kp_ref[:, ls], kc_ref[0:blk, ls]], axis=0)
                v2 = jnp.concatenate([vp_ref[:, ls], vc_ref[0:blk, ls]], axis=0)
            else:
                k2 = kc_ref[(c - 1) * blk:(c + 1) * blk, ls]
                v2 = vc_ref[(c - 1) * blk:(c + 1) * blk, ls]
            s = jnp.where(mask2, _dot_nt(qs, k2), NEG_INF)
            m = jnp.max(s, axis=-1, keepdims=True)
            p = jnp.exp(s - m)
            den = jnp.sum(p, axis=-1, keepdims=True)
            o = _dot(p.astype(BF16), v2) / den
            lse = m + jnp.log(den)
            out_ref[rows, ls] = jnp.where(even_head, o[0:blk], o[blk:2 * blk]).astype(BF16)
            lse_blk = jnp.where(lane_head == 2 * hp, lse[0:blk],
                                jnp.where(lane_head == 2 * hp + 1, lse[blk:2 * blk], lse_blk))
        lse_ref[rows, :] = lse_blk


def _attn_group(qkv_g, g, nb, seq, side=None):
    dil = DIL_PATTERNS[g][1]
    sub = seq // dil
    tq, blk = ATT_TQ, ATT_BLK
    nq = sub // tq
    parts = GROUP_W // A_WIDTH
    cur = lambda part: pl.BlockSpec((None, tq, A_WIDTH), lambda b, r, n: (b, n, r * parts + part))
    prev = lambda part: pl.BlockSpec((None, blk, A_WIDTH),
                                     lambda b, r, n: (b, jnp.maximum(n * (tq // blk) - 1, 0), r * parts + part))
    in_specs = [cur(0), cur(1), prev(1), cur(2), prev(2)]
    args = [qkv_g] * 5
    o_spec = pl.BlockSpec((None, tq, A_WIDTH), lambda b, r, n: (b, n, r))
    l_spec = pl.BlockSpec((None, tq, V7X_LANES), lambda b, r, n: (b, n, r))
    out_specs = [o_spec, l_spec]
    out_shape = [jax.ShapeDtypeStruct((nb, sub, dil * A_WIDTH), BF16),
                 jax.ShapeDtypeStruct((nb, sub, dil * V7X_LANES), F32)]
    vm = 2 * (10 * _nbytes((tq, A_WIDTH), BF16) + 3 * _nbytes((tq, V7X_LANES), F32))
    side_seg = 0
    if side is not None:
        s_qkv, caches_t, jc, first_seq, n_seq, side_seg = side
        steps = nb * dil * nq
        n = n_seq // steps
        assert n * steps == n_seq and first_seq % n == 0 and first_seq + n_seq <= caches_t[0].shape[1]
        step = lambda b, r, i: (b * dil + r) * nq + i
        in_specs += ([pl.BlockSpec((n * side_seg, QKV_W), lambda b, r, i: (first_seq // n + step(b, r, i), 0))]
                     + [pl.BlockSpec((None, n, KV_W, c.shape[3]),
                                     lambda b, r, i: (jc, first_seq // n + step(b, r, i), 0, 0)) for c in caches_t])
        args += [s_qkv, *caches_t]
        out_specs.append(pl.BlockSpec((n * side_seg, A_WIDTH), lambda b, r, i: (step(b, r, i), 0)))
        out_shape.append(jax.ShapeDtypeStruct((n_seq * side_seg, A_WIDTH), F32))
        cache_bytes = n * sum(_nbytes(c.shape[2:], F32) for c in caches_t)
        vm += (2 * (cache_bytes + _nbytes((n * side_seg, QKV_W), F32))
               + 16 * _nbytes((A_HEADS * side_seg, ATT_S_CHUNK), F32) + 4 * _nbytes((A_WIDTH, ATT_S_CHUNK), F32))
    res = pl.pallas_call(
        functools.partial(_attn_kernel, side_seg=side_seg),
        grid=(nb, dil, nq),
        in_specs=in_specs,
        out_specs=out_specs,
        out_shape=out_shape,
        compiler_params=_cparams(3, vm),
        name=f"attn_d{dil}",
    )(*args)
    return tuple(res)


def _attn_s_body(qkv_ref, c_refs, out_ref, seg):
    hq = A_HEADS * seg
    row = lax.broadcasted_iota(jnp.int32, (hq, A_WIDTH), 0)
    lane = lax.broadcasted_iota(jnp.int32, (hq, A_WIDTH), 1)
    head_lanes = (lane // HEAD_DIM) == (row // seg)
    out_lane = lax.broadcasted_iota(jnp.int32, (seg, A_WIDTH), 1) // HEAD_DIM
    pad = V7X_LANES - seg
    for s in range(c_refs[0].shape[0]):
        new = qkv_ref[s * seg:(s + 1) * seg, :]
        scores, values = [], []
        for g, ((win, dil), c_ref) in enumerate(zip(DIL_PATTERNS, c_refs)):
            base = g * GROUP_W
            q = new[:, base:base + A_WIDTH]
            qrows = jnp.where(head_lanes, jnp.concatenate([q] * A_HEADS, axis=0), 0.0).astype(BF16)
            nk = c_ref.shape[2]
            ch = min(nk, ATT_S_CHUNK)
            tok = lax.broadcasted_iota(jnp.int32, (hq, ch), 0) % seg
            key = lax.broadcasted_iota(jnp.int32, (hq, ch), 1)
            same_class = (key % dil) == (tok % dil)
            for lo in range(0, nk, ch):
                valid = same_class & (key >= tok) if lo < seg else same_class
                kt = c_ref[s, 0:A_WIDTH, lo:lo + ch].astype(BF16)
                scores.append(jnp.where(valid, _dot(qrows, kt), NEG_INF))
                values.append((c_ref, lo, ch))
            kn = jnp.concatenate([new[:, base + A_WIDTH:base + 2 * A_WIDTH], jnp.zeros((pad, A_WIDTH), F32)], axis=0)
            vn = jnp.concatenate([new[:, base + 2 * A_WIDTH:base + 3 * A_WIDTH], jnp.zeros((pad, A_WIDTH), F32)], axis=0)
            tok_n = lax.broadcasted_iota(jnp.int32, (hq, V7X_LANES), 0) % seg
            key_n = lax.broadcasted_iota(jnp.int32, (hq, V7X_LANES), 1)
            valid_n = (key_n <= tok_n) & (((tok_n - key_n) % dil) == 0)
            scores.append(jnp.where(valid_n, _dot_nt(qrows, kn.astype(BF16)), NEG_INF))
            values.append(vn.astype(BF16))
        m = scores[0].max(axis=-1, keepdims=True)
        for sc in scores[1:]:
            m = jnp.maximum(m, sc.max(axis=-1, keepdims=True))
        den = jnp.zeros((hq, 1), F32)
        acc = jnp.zeros((hq, A_WIDTH), F32)
        for sc, val in zip(scores, values):
            p = jnp.exp(sc - m)
            den = den + p.sum(axis=-1, keepdims=True)
            if isinstance(val, tuple):
                c_ref, lo, ch = val
                acc = acc + _dot_nt(p.astype(BF16), c_ref[s, A_WIDTH:KV_W, lo:lo + ch].astype(BF16))
            else:
                acc = acc + _dot(p.astype(BF16), val)
        acc = acc / den
        o = jnp.zeros((seg, A_WIDTH), F32)
        for hh in range(A_HEADS):
            o = jnp.where(out_lane == hh, acc[hh * seg:(hh + 1) * seg], o)
        out_ref[s * seg:(s + 1) * seg, :] = o


def _rope_tables(pos):
    half = HEAD_DIM // 2
    inv = jnp.power(ROPE_THETA, -jnp.arange(half, dtype=F32) / half)
    reps = V7X_LANES // half
    ang = pos.astype(F32)[:, None] * jnp.tile(inv, reps)[None, :]
    sign = jnp.tile(jnp.concatenate([-jnp.ones((half,), F32), jnp.ones((half,), F32)]), reps // 2)
    return jnp.cos(ang), jnp.sin(ang) * sign[None, :]


def _prompt_mixers(w, nb, seq):
    tm = MIXER_ROW_TILE
    nt = seq // tm
    cos, sin = _rope_tables(jnp.arange(seq, dtype=jnp.int32))
    dils = tuple(d for _, d in DIL_PATTERNS)
    win_tot = tuple(min(win, seq) for win, _ in DIL_PATTERNS)
    win_r = tuple(min(wt, tm) for wt in win_tot)
    win_first = tuple(nt - max(wt // tm, 1) for wt in win_tot)
    win_cfg = dict(
        win_shapes=[(nb, KV_W, wt) for wt in win_tot],
        win_blocks=[(1, KV_W, r) for r in win_r],
        win_index=[(lambda b, t, first=first: (b, 0, jnp.maximum(t - first, 0))) for first in win_first],
        win_first=win_first,
        win_chunks=tuple(((0, tm - r, r),) for r in win_r))
    pool_halo = jnp.zeros((nb, _hist_rows(POOL_STATE, 1), D_POOL), F32)
    conv_halo = jnp.zeros((nb, _hist_rows(CONV_W - 1, 1), D_MODEL), F32)

    def ab_mixer(x, j, sample):
        q0, q1, q2, p, w0, w1, w2, ptail = _ab_in_call(
            x, w, j, cos, sin, pool_halo, nb=nb, nt=nt, tm=tm, stride=1, dils=dils, pos0=0,
            tail_rows=pool_halo.shape[1], tail_per_tile=False, **win_cfg)
        s_qkv, caches_t, nseq, seg = sample
        half = nseq // 2
        o2, l2, att_lo = _attn_group(q2, 2, nb, seq, side=(s_qkv, caches_t, j, 0, half, seg))
        o1, l1, att_hi = _attn_group(q1, 1, nb, seq, side=(s_qkv, caches_t, j, half, nseq - half, seg))
        o0, l0 = _attn_group(q0, 0, nb, seq)
        groups = [(o0, l0, dils[0]), (o1, l1, dils[1]), (o2, l2, dils[2])]
        return (groups, p, (w0, w1, w2), ptail[:, ptail.shape[1] - POOL_STATE:],
                jnp.concatenate([att_lo, att_hi], axis=0))

    def conv_mixer(x, j):
        x, ctail = _conv_layer(x, w["mix_norm_c"], w["conv_w_in"], w["conv_w"], w["conv_w_out"], j, conv_halo,
                               nb=nb, nt=nt, tm=tm, stride=1, tail_rows=conv_halo.shape[1], tail_per_tile=False)
        return x, ctail[:, ctail.shape[1] - (CONV_W - 1):]

    def present_windows(wins):
        out = []
        for g in range(N_DIL):
            wt = jnp.stack(wins[g], axis=0).reshape(len(wins[g]), nb, 2, A_HEADS, HEAD_DIM, win_tot[g])
            out.append(jnp.transpose(wt, (0, 1, 5, 2, 3, 4)))
        return out

    return ab_mixer, conv_mixer, present_windows


def _ab_in_call(x, w, j, cos, sin, halo, **kw):
    return _ab_in(x, w["mix_norm_ab"], w["ab_w_in"], cos, sin, w["pool_w"], w["pool_scale"], j, halo, **kw)


def _sample_mixers(w, nseq, seg, pool_state, conv_state):
    tm = SAMPLE_ROW_TILE
    rows = seg * nseq
    nt = rows // tm
    tok_tile = tm // nseq
    cos, sin = _rope_tables(PAST_LEN + jnp.arange(seg, dtype=jnp.int32))
    cos, sin = jnp.repeat(cos, nseq, axis=0), jnp.repeat(sin, nseq, axis=0)
    win_cfg = dict(
        win_shapes=[(seg, KV_W, nseq)] * N_DIL,
        win_blocks=[(tok_tile, KV_W, nseq)] * N_DIL,
        win_index=[lambda b, t: (t, 0, 0)] * N_DIL,
        win_first=(0,) * N_DIL,
        win_chunks=(tuple((i, i * nseq, nseq) for i in range(tok_tile)),) * N_DIL)

    def ab_in(x, j):
        state = jnp.transpose(pool_state[j], (1, 0, 2))
        halo = state.reshape(1, POOL_STATE * nseq, D_POOL)
        q0, q1, q2, p, w0, w1, w2, u = _ab_in_call(
            x, w, j, cos, sin, halo, nb=1, nt=nt, tm=tm, stride=nseq, dils=(1,) * N_DIL, pos0=PAST_LEN,
            tail_rows=tm, tail_per_tile=True, **win_cfg)
        qkv = jnp.concatenate([q.reshape(seg, nseq, GROUP_W) for q in (q0, q1, q2)], axis=-1)
        qkv = jnp.transpose(qkv, (1, 0, 2)).reshape(rows, QKV_W).astype(F32)
        u = u.reshape(seg, nseq, D_POOL)
        pool = jnp.transpose(jnp.concatenate([state, u], axis=0)[-POOL_STATE:], (1, 0, 2))
        return qkv, p, (w0, w1, w2), pool

    def attn_rows(a):
        return jnp.transpose(a.reshape(nseq, seg, A_WIDTH), (1, 0, 2)).reshape(rows, A_WIDTH).astype(BF16)

    def conv_mixer(x, j):
        state = jnp.transpose(conv_state[j], (1, 0, 2))
        halo = state.reshape(1, (CONV_W - 1) * nseq, D_MODEL)
        x, cu = _conv_layer(x, w["mix_norm_c"], w["conv_w_in"], w["conv_w"], w["conv_w_out"], j, halo,
                            nb=1, nt=nt, tm=tm, stride=nseq, tail_rows=tm, tail_per_tile=True)
        cu = cu.reshape(seg, nseq, D_MODEL)
        return x, jnp.transpose(jnp.concatenate([state, cu], axis=0)[-(CONV_W - 1):], (1, 0, 2))

    def present_windows(wins):
        out = []
        for g in range(N_DIL):
            wt = jnp.stack(wins[g], axis=0).reshape(len(wins[g]), seg, 2, A_HEADS, HEAD_DIM, nseq)
            out.append(jnp.transpose(wt, (0, 5, 1, 2, 3, 4)))
        return out

    return ab_in, attn_rows, conv_mixer, present_windows


def _trunks(xp, xs, w, caches, pool_state, conv_state, *, nb, seq, nseq, seg):
    depth = w["ffn1_norm"].shape[0]
    p_ab, p_conv, p_windows = _prompt_mixers(w, nb, seq)
    s_ab_in, s_attn_rows, s_conv, s_windows = _sample_mixers(w, nseq, seg, pool_state, conv_state)
    caches_t = [jnp.transpose(c, (0, 1, 3, 4, 5, 2)).reshape(c.shape[0], nseq, KV_W, c.shape[2]) for c in caches]
    final_w = lambda l: w["final_norm"] if l == depth - 1 else None
    ffn1_both = lambda xa, xb, l: _ffn_pair(xa, xb, w["ffn1_norm"], w["ffn1_w_gu"], w["ffn1_w_down"], l)
    ffn2_both = lambda xa, xb, l: _ffn_pair(xa, xb, w["ffn2_norm"], w["ffn2_w_gu"], w["ffn2_w_down"], l,
                                            final_w=final_w(l))
    ffn2 = lambda x, l, **kw: _ffn(x, w["ffn2_norm"], w["ffn2_w_gu"], w["ffn2_w_down"], l, final_w=final_w(l), **kw)
    wins_p, wins_s = [[] for _ in range(N_DIL)], [[] for _ in range(N_DIL)]
    pools_p, pools_s, convs_p, convs_s = [], [], [], []
    for l in range(depth):
        j = l // 2
        xp, xs = ffn1_both(xp, xs, l)
        if l % 2 == 0:
            qkv_s, p_s, win_s, pool_s = s_ab_in(xs, j)
            a_p, p_p, win_p, pool_p, att_s = p_ab(xp, j, (qkv_s, caches_t, nseq, seg))
            xp = ffn2(xp, l, premix=(a_p, p_p, w["ab_w_out"], j))
            xs = ffn2(xs, l, premix=(s_attn_rows(att_s), p_s, w["ab_w_out"], j))
            for g in range(N_DIL):
                wins_p[g].append(win_p[g])
                wins_s[g].append(win_s[g])
            pools_p.append(pool_p)
            pools_s.append(pool_s)
        else:
            xp, conv_p = p_conv(xp, j)
            xs, conv_s = s_conv(xs, j)
            xp, xs = ffn2_both(xp, xs, l)
            convs_p.append(conv_p)
            convs_s.append(conv_s)
    return ((xp, p_windows(wins_p), jnp.stack(pools_p, axis=0), jnp.stack(convs_p, axis=0)),
            (xs, s_windows(wins_s), jnp.stack(pools_s, axis=0), jnp.stack(convs_s, axis=0)))


def kernel(x_prompt, x_sample, cache_win0, cache_win1, cache_win2, state_pool, state_conv, ffn1_norm, ffn1_w_gu, ffn1_w_down, mix_norm, ffn2_norm, ffn2_w_gu, ffn2_w_down, ab_w_in, ab_w_out, pool_w, pool_scale, conv_w_in, conv_w, conv_w_out, final_norm):
    batch, seq, _ = x_prompt.shape
    dbatch, dseq, _ = x_sample.shape
    assert seq % ROW_TILE == 0 and (dbatch * dseq) % ROW_TILE == 0 and dbatch % 2 == 0
    assert SAMPLE_ROW_TILE % dbatch == 0 and (dbatch * dseq) % SAMPLE_ROW_TILE == 0 and dbatch % V7X_SUBLANES == 0
    assert all((seq // dil) % ATT_TQ == 0 and ROW_TILE % dil == 0 for _, dil in DIL_PATTERNS)
    assert dseq == V7X_SUBLANES and PAST_LEN + 1 >= max(POOL_WINDOWS) and PAST_LEN >= POOL_STATE
    assert all(c.shape[2] == win and win == dil * SPAN for c, (win, dil) in zip((cache_win0, cache_win1, cache_win2), DIL_PATTERNS))
    row3 = lambda a: a.reshape(a.shape[0], 1, a.shape[1])
    w = dict(
        ffn1_norm=row3(ffn1_norm), ffn1_w_gu=ffn1_w_gu.astype(BF16), ffn1_w_down=ffn1_w_down.astype(BF16),
        ffn2_norm=row3(ffn2_norm), ffn2_w_gu=ffn2_w_gu.astype(BF16), ffn2_w_down=ffn2_w_down.astype(BF16),
        mix_norm_ab=row3(mix_norm[0::2]), mix_norm_c=row3(mix_norm[1::2]),
        ab_w_in=ab_w_in.astype(BF16), ab_w_out=ab_w_out.astype(BF16), pool_w=pool_w.astype(BF16), pool_scale=row3(pool_scale),
        conv_w_in=conv_w_in.astype(BF16), conv_w=conv_w, conv_w_out=conv_w_out.astype(BF16), final_norm=final_norm)
    xs = jnp.transpose(x_sample, (1, 0, 2)).reshape(dseq * dbatch, D_MODEL)
    (y_p, win_p, pool_p, conv_p), (y_s, win_s, pool_s, conv_s) = _trunks(
        x_prompt.reshape(batch * seq, D_MODEL), xs, w, (cache_win0, cache_win1, cache_win2), state_pool, state_conv,
        nb=batch, seq=seq, nseq=dbatch, seg=dseq)
    y_s = jnp.transpose(y_s.reshape(dseq, dbatch, D_MODEL), (1, 0, 2))
    return (y_p.reshape(batch, seq, D_MODEL), y_s,
            win_p[0], win_p[1], win_p[2], pool_p, conv_p,
            win_s[0], win_s[1], win_s[2], pool_s, conv_s)
```

```python
import functools

import jax
import jax.numpy as jnp
from jax import lax
from jax.experimental import pallas as pl
from jax.experimental.pallas import tpu as pltpu

F32 = jnp.float32
BF16 = jnp.bfloat16

D_MODEL = 1024
HEAD_DIM = 64
A_HEADS = 8
A_WIDTH = A_HEADS * HEAD_DIM
DIL_PATTERNS = ((128, 1), (512, 4), (2048, 16))
N_DIL = len(DIL_PATTERNS)
SPAN = 128
GROUP_W = 3 * A_WIDTH
QKV_W = N_DIL * GROUP_W
KV_W = 2 * A_WIDTH
POOL_WINDOWS = (2, 4, 8, 16)
D_POOL = 512
POOL_GC = 128
POOL_STATE = 15
CONV_W = 3
D_FF = 2816
ROPE_THETA = 10000.0
EPS = 1e-6
PAST_LEN = 2048

V7X_LANES = 128
V7X_SUBLANES = 8
V7X_MXU_DIM = 256
V7X_VMEM_USABLE = 56 * 1024 * 1024
MIB = 1024 * 1024

ROW_TILE = 512
MIXER_ROW_TILE = 512
SAMPLE_ROW_TILE = 256
ATT_S_CHUNK = 512
ATT_TQ = 256
ATT_BLK = SPAN
LSE_LANES = V7X_LANES // A_HEADS
LANE_CHUNKS = A_WIDTH // V7X_LANES
NEG_INF = float("-inf")


def _round_up(n, m):
    return -(-n // m) * m


def _nbytes(shape, dtype):
    n = 1
    for s in shape:
        n *= s
    return n * jnp.dtype(dtype).itemsize


def _cparams(n_axes, block_bytes):
    limit = int(min(V7X_VMEM_USABLE, block_bytes * 1.25 + 12 * MIB))
    return pltpu.CompilerParams(dimension_semantics=("arbitrary",) * n_axes, vmem_limit_bytes=limit)


def _layer_block(shape, layer):
    nd = len(shape)
    return pl.BlockSpec((None,) + tuple(shape), lambda *_: (layer,) + (0,) * nd, pipeline_mode=pl.Buffered(1))


def _const_block(shape):
    nd = len(shape)
    return pl.BlockSpec(tuple(shape), lambda *_: (0,) * nd, pipeline_mode=pl.Buffered(1))


def _rms(x, w):
    return x * lax.rsqrt(jnp.mean(x * x, axis=-1, keepdims=True) + EPS) * w


def _dot(a, b):
    return jnp.dot(a, b, preferred_element_type=F32)


def _dot_nt(a, b):
    return lax.dot_general(a, b, (((1,), (1,)), ((), ())), preferred_element_type=F32)


def _merge_groups(group_refs, dils, a_scr, regroup, tm):
    blk = ATT_BLK
    lane = lax.broadcasted_iota(jnp.int32, (blk, V7X_LANES), 1)
    even_head = lane < HEAD_DIM
    for (o_ref, l_ref), d, (o_n, l_n) in zip(group_refs, dils, regroup):
        for r in range(d):
            dst = pl.ds(r, tm // d, stride=d) if d > 1 else slice(0, tm)
            for hp in range(LANE_CHUNKS):
                lo = r * A_WIDTH + hp * V7X_LANES
                o_n[hp, dst, :] = o_ref[:, lo:lo + V7X_LANES].astype(F32)
            l_n[dst, :] = l_ref[:, r * V7X_LANES:(r + 1) * V7X_LANES]
    for c in range(tm // blk):
        rows = slice(c * blk, (c + 1) * blk)
        lses = [l_n[rows, :] for _, l_n in regroup]
        mx = functools.reduce(jnp.maximum, lses)
        es = [jnp.exp(l - mx) for l in lses]
        tot = functools.reduce(jnp.add, es)
        ws = [e / tot for e in es]
        for hp in range(LANE_CHUNKS):
            ce, co = 2 * hp * LSE_LANES, (2 * hp + 1) * LSE_LANES
            acc = None
            for w, (o_n, _) in zip(ws, regroup):
                term = jnp.where(even_head, w[:, ce:ce + 1], w[:, co:co + 1]) * o_n[hp, rows, :]
                acc = term if acc is None else acc + term
            a_scr[rows, hp * V7X_LANES:(hp + 1) * V7X_LANES] = acc.astype(BF16)


def _ffn_kernel(*refs, premix, final, merge_dils, tm):
    refs = list(refs)
    x_ref, nw_ref, wgu_ref, wd_ref = refs[:4]
    del refs[:4]
    x = x_ref[...]
    if premix == "rows":
        a_ref, p_ref, wo_ref = refs[:3]
        del refs[:3]
        x = x + _dot(a_ref[...], wo_ref[0:A_WIDTH, :]) + _dot(p_ref[...], wo_ref[A_WIDTH:A_WIDTH + D_POOL, :])
    elif premix == "merge":
        ng = len(merge_dils)
        cur_refs = [(refs[2 * g], refs[2 * g + 1]) for g in range(ng)]
        next_refs = [(refs[2 * ng + 2 * g], refs[2 * ng + 2 * g + 1]) for g in range(ng)]
        p_ref, wo_ref = refs[4 * ng:4 * ng + 2]
        del refs[:4 * ng + 2]
        mix_scr = refs[-1 - 2 * ng]
        regroup = [(refs[-2 * ng + 2 * g], refs[-2 * ng + 2 * g + 1]) for g in range(ng)]
        del refs[-1 - 2 * ng:]
        pl.when(pl.program_id(0) == 0)(lambda: _merge_groups(cur_refs, merge_dils, mix_scr, regroup, tm))
        x = x + _dot(mix_scr[...], wo_ref[0:A_WIDTH, :]) + _dot(p_ref[...], wo_ref[A_WIDTH:A_WIDTH + D_POOL, :])
    fw_ref = refs.pop(0) if final else None
    o_ref, a_scr = refs
    o_ref[...] = _ffn_tile(x, nw_ref, wgu_ref, wd_ref, a_scr, fw_ref)
    if premix == "merge":
        _merge_groups(next_refs, merge_dils, mix_scr, regroup, tm)


def _ffn_tile(x, nw_ref, wgu_ref, wd_ref, a_scr, fw_ref):
    h = _rms(x, nw_ref[...]).astype(BF16)
    for lo in range(0, D_FF, V7X_MXU_DIM):
        g = _dot(h, wgu_ref[:, lo:lo + V7X_MXU_DIM])
        u = _dot(h, wgu_ref[:, D_FF + lo:D_FF + lo + V7X_MXU_DIM])
        a_scr[:, lo:lo + V7X_MXU_DIM] = (g * jax.nn.sigmoid(g) * u).astype(BF16)
    y = x + 0.5 * _dot(a_scr[...], wd_ref[...])
    return y if fw_ref is None else _rms(y, fw_ref[...])


def _ffn_pair_kernel(xa_ref, xb_ref, nw_ref, wgu_ref, wd_ref, *rest, final, steps_a):
    fw_ref = rest[0] if final else None
    oa_ref, ob_ref, a_scr = rest[-3:]
    i = pl.program_id(0)

    @pl.when(i < steps_a)
    def _():
        oa_ref[...] = _ffn_tile(xa_ref[...], nw_ref, wgu_ref, wd_ref, a_scr, fw_ref)

    @pl.when(i >= steps_a)
    def _():
        ob_ref[...] = _ffn_tile(xb_ref[...], nw_ref, wgu_ref, wd_ref, a_scr, fw_ref)


def _ffn_pair(xa, xb, nw, wgu, wd, layer, *, final_w=None):
    tm = ROW_TILE
    steps_a, steps_b = xa.shape[0] // tm, xb.shape[0] // tm
    final = final_w is not None
    a_index = lambda i: (jnp.minimum(i, steps_a - 1), 0)
    b_index = lambda i: (jnp.maximum(i - steps_a, 0), 0)
    in_specs = [pl.BlockSpec((tm, D_MODEL), a_index), pl.BlockSpec((tm, D_MODEL), b_index),
                _layer_block((1, D_MODEL), layer), _layer_block((D_MODEL, 2 * D_FF), layer),
                _layer_block((D_FF, D_MODEL), layer)]
    args = [xa, xb, nw, wgu, wd]
    if final:
        in_specs.append(_const_block((1, D_MODEL)))
        args.append(final_w.reshape(1, D_MODEL))
    vm = (8 * _nbytes((tm, D_MODEL), F32) + _nbytes((D_MODEL, 2 * D_FF), BF16) + _nbytes((D_FF, D_MODEL), BF16)
          + _nbytes((tm, D_FF), BF16))
    return pl.pallas_call(
        functools.partial(_ffn_pair_kernel, final=final, steps_a=steps_a),
        grid=(steps_a + steps_b,),
        in_specs=in_specs,
        out_specs=[pl.BlockSpec((tm, D_MODEL), a_index), pl.BlockSpec((tm, D_MODEL), b_index)],
        out_shape=[jax.ShapeDtypeStruct(xa.shape, F32), jax.ShapeDtypeStruct(xb.shape, F32)],
        scratch_shapes=[pltpu.VMEM((tm, D_FF), BF16)],
        compiler_params=_cparams(1, vm),
        name="ffn_pair" + ("_final" if final else ""),
    )(*args)


def _ffn(x, nw, wgu, wd, layer, *, final_w=None, premix=None):
    rows = x.shape[0]
    tm = ROW_TILE
    final = final_w is not None
    in_specs = [pl.BlockSpec((tm, D_MODEL), lambda i: (i, 0)),
                _layer_block((1, D_MODEL), layer), _layer_block((D_MODEL, 2 * D_FF), layer),
                _layer_block((D_FF, D_MODEL), layer)]
    args = [x, nw, wgu, wd]
    vm = (4 * _nbytes((tm, D_MODEL), F32) + _nbytes((D_MODEL, 2 * D_FF), BF16) + _nbytes((D_FF, D_MODEL), BF16)
          + _nbytes((tm, D_FF), BF16))
    scratch = [pltpu.VMEM((tm, D_FF), BF16)]
    mode, merge_dils = None, ()
    if premix is not None:
        a, p, w_out, j = premix
        if isinstance(a, (list, tuple)):
            mode, merge_dils = "merge", tuple(d for _, _, d in a)
            scratch.append(pltpu.VMEM((tm, A_WIDTH), BF16))
            last = rows // tm - 1
            for index in (lambda i: (i, 0), lambda i: (jnp.minimum(i + 1, last), 0)):
                for o, lse, d in a:
                    assert tm % (d * 2 * V7X_SUBLANES) == 0
                    in_specs += [pl.BlockSpec((tm // d, d * A_WIDTH), index),
                                 pl.BlockSpec((tm // d, d * V7X_LANES), index)]
                    args += [o.reshape(rows // d, d * A_WIDTH), lse.reshape(rows // d, d * V7X_LANES)]
                    vm += 2 * _nbytes((tm, A_WIDTH), BF16) + 2 * _nbytes((tm, V7X_LANES), F32)
            for _ in a:
                scratch += [pltpu.VMEM((LANE_CHUNKS, tm, V7X_LANES), F32), pltpu.VMEM((tm, V7X_LANES), F32)]
                vm += _nbytes((tm, A_WIDTH + V7X_LANES), F32)
        else:
            mode = "rows"
            in_specs.append(pl.BlockSpec((tm, A_WIDTH), lambda i: (i, 0)))
            args.append(a)
        in_specs += [pl.BlockSpec((tm, D_POOL), lambda i: (i, 0)), _layer_block((A_WIDTH + D_POOL, D_MODEL), j)]
        args += [p, w_out]
        vm += 4 * _nbytes((tm, A_WIDTH), BF16) + _nbytes((D_MODEL, D_MODEL), BF16)
    if final:
        in_specs.append(_const_block((1, D_MODEL)))
        args.append(final_w.reshape(1, D_MODEL))
    return pl.pallas_call(
        functools.partial(_ffn_kernel, premix=mode, final=final, merge_dils=merge_dils, tm=tm),
        grid=(rows // tm,),
        in_specs=in_specs,
        out_specs=pl.BlockSpec((tm, D_MODEL), lambda i: (i, 0)),
        out_shape=jax.ShapeDtypeStruct((rows, D_MODEL), F32),
        scratch_shapes=scratch,
        compiler_params=_cparams(1, vm),
        name="ffn" + {None: "", "rows": "_mix", "merge": "_merge"}[mode] + ("_final" if final else ""),
    )(*args)


def _hist_rows(positions, stride):
    return _round_up(positions * stride, V7X_SUBLANES)


def _init_history(ext, halo_ref, hist):
    @pl.when(pl.program_id(1) == 0)
    def _():
        ext[0:hist, :] = halo_ref[...]


def _carry_history(ext, hist, tm):
    ext[0:hist, :] = ext[tm:tm + hist, :]


def _conv_kernel(x_ref, nw_ref, win_ref, cw_ref, wout_ref, halo_ref, o_ref, tail_ref, ext, *, tm, stride, hist, tail_rows):
    _init_history(ext, halo_ref, hist)
    x = x_ref[...]
    h = _rms(x, nw_ref[...]).astype(BF16)
    gb = _dot(h, win_ref[:, 0:D_MODEL])
    gc = _dot(h, win_ref[:, D_MODEL:2 * D_MODEL])
    v = _dot(h, win_ref[:, 2 * D_MODEL:3 * D_MODEL])
    ext[hist:hist + tm, :] = gc * v
    cw = cw_ref[...]
    y = cw[CONV_W - 1] * ext[hist:hist + tm, :]
    for back in range(1, CONV_W):
        lo = hist - back * stride
        y = y + cw[CONV_W - 1 - back] * ext[lo:lo + tm, :]
    o_ref[...] = x + _dot((gb * y).astype(BF16), wout_ref[...])
    tail_ref[...] = ext[hist + tm - tail_rows:hist + tm, :]
    _carry_history(ext, hist, tm)


def _tail_spec(tail_rows, width, nt, per_tile):
    if per_tile:
        return nt * tail_rows, pl.BlockSpec((None, tail_rows, width), lambda b, t: (b, t, 0))
    return tail_rows, pl.BlockSpec((None, tail_rows, width), lambda b, t: (b, 0, 0))


def _conv_layer(x, nw, w_in, cw, w_out, layer, halo, *, nb, nt, tm, stride, tail_rows, tail_per_tile):
    hist = _hist_rows(CONV_W - 1, stride)
    assert halo.shape == (nb, hist, D_MODEL)
    rows = x.shape[0]
    tail_total, tail_spec = _tail_spec(tail_rows, D_MODEL, nt, tail_per_tile)
    vm = (8 * _nbytes((tm, D_MODEL), F32) + _nbytes((D_MODEL, 4 * D_MODEL), BF16)
          + _nbytes((hist + tm, D_MODEL), F32) + 4 * _nbytes((hist + tail_rows, D_MODEL), F32))
    return pl.pallas_call(
        functools.partial(_conv_kernel, tm=tm, stride=stride, hist=hist, tail_rows=tail_rows),
        grid=(nb, nt),
        in_specs=[pl.BlockSpec((tm, D_MODEL), lambda b, t: (b * nt + t, 0)),
                  _layer_block((1, D_MODEL), layer), _layer_block((D_MODEL, 3 * D_MODEL), layer),
                  _layer_block((CONV_W, D_MODEL), layer), _layer_block((D_MODEL, D_MODEL), layer),
                  pl.BlockSpec((None, hist, D_MODEL), lambda b, t: (b, 0, 0))],
        out_specs=[pl.BlockSpec((tm, D_MODEL), lambda b, t: (b * nt + t, 0)), tail_spec],
        out_shape=[jax.ShapeDtypeStruct((rows, D_MODEL), F32),
                   jax.ShapeDtypeStruct((nb, tail_total, D_MODEL), F32)],
        scratch_shapes=[pltpu.VMEM((hist + tm, D_MODEL), F32)],
        compiler_params=_cparams(2, vm),
        name="conv_layer",
    )(x, nw, w_in, cw, w_out, halo)


def _ab_in_kernel(x_ref, nw_ref, w_ref, cos_ref, sin_ref, pw_ref, ps_ref, halo_ref,
                  q0_ref, q1_ref, q2_ref, p_ref, win0_ref, win1_ref, win2_ref, tail_ref, ext, perm,
                  *, tm, stride, dils, pos0, hist, tail_rows, win_first, win_chunks):
    qkv_refs = (q0_ref, q1_ref, q2_ref)
    win_refs = (win0_ref, win1_ref, win2_ref)
    t = pl.program_id(1)
    _init_history(ext, halo_ref, hist)
    h = _rms(x_ref[...], nw_ref[...]).astype(BF16)

    cos, sin = cos_ref[...], sin_ref[...]
    lane = lax.broadcasted_iota(jnp.int32, (tm, V7X_LANES), 1)
    first_half = (lane % HEAD_DIM) < (HEAD_DIM // 2)

    def rope(z):
        out = []
        for c in range(z.shape[1] // V7X_LANES):
            zc = z[:, c * V7X_LANES:(c + 1) * V7X_LANES]
            partner = jnp.where(first_half, pltpu.roll(zc, V7X_LANES - HEAD_DIM // 2, 1), pltpu.roll(zc, HEAD_DIM // 2, 1))
            out.append(zc * cos + partner * sin)
        return jnp.concatenate(out, axis=1)

    piece = V7X_MXU_DIM
    slabs = piece // V7X_LANES
    kv = []
    for g in range(N_DIL):
        base, d = g * GROUP_W, dils[g]

        def emit(col, z, g=g, d=d):
            out = qkv_refs[g]
            if d == 1:
                out[:, col:col + piece] = z.astype(BF16)
                return
            for c in range(slabs):
                perm[c] = z[:, c * V7X_LANES:(c + 1) * V7X_LANES]
            for r in range(d):
                rows_r = [perm[c, pl.ds(r, tm // d, stride=d), :] for c in range(slabs)]
                out[:, r * GROUP_W + col:r * GROUP_W + col + piece] = jnp.concatenate(rows_r, axis=1).astype(BF16)

        k_pieces, v_pieces = [], []
        for col in range(0, GROUP_W, piece):
            z = _dot(h, w_ref[:, base + col:base + col + piece])
            if col < A_WIDTH:
                z = rope(z) * (HEAD_DIM ** -0.5)
            elif col < 2 * A_WIDTH:
                z = rope(z)
                k_pieces.append(z)
            else:
                v_pieces.append(z)
            emit(col, z)
        kv.append((k_pieces, v_pieces))

    ext[hist:hist + tm, :] = _dot(h, w_ref[:, QKV_W:QKV_W + D_POOL])
    if pos0 + 1 >= max(POOL_WINDOWS):
        pos = None
    else:
        pos = pos0 + t * (tm // stride) + lax.broadcasted_iota(jnp.int32, (tm, 1), 0) // stride
    for gi, w in enumerate(POOL_WINDOWS):
        sl = slice(gi * POOL_GC, (gi + 1) * POOL_GC)
        tok = ext[hist:hist + tm, sl]
        acc = tok
        for back in range(1, w):
            lo = hist - back * stride
            acc = acc + ext[lo:lo + tm, sl]
        cnt = float(w) if pos is None else jnp.minimum(w, pos + 1).astype(F32)
        d_tok = acc / cnt - tok
        p_ref[:, sl] = (_dot(d_tok.astype(BF16), pw_ref[gi]) * ps_ref[:, sl]).astype(BF16)
    tail_ref[...] = ext[hist + tm - tail_rows:hist + tm, :]
    _carry_history(ext, hist, tm)

    for g, (k_pieces, v_pieces) in enumerate(kv):
        def write_rows(g=g, pieces=k_pieces + v_pieces):
            for dst, lo, width in win_chunks[g]:
                for i, z in enumerate(pieces):
                    win_refs[g][dst, i * piece:(i + 1) * piece, :] = z[lo:lo + width].T

        if win_first[g] == 0:
            write_rows()
        else:
            pl.when(t >= win_first[g])(write_rows)


def _ab_in(x, nw, w_in, cos, sin, pool_w, pool_scale, layer, halo, *, nb, nt, tm, stride, dils, pos0, tail_rows,
           tail_per_tile, win_shapes, win_blocks, win_index, win_first, win_chunks):
    rows = x.shape[0]
    hist = _hist_rows(POOL_STATE, stride)
    assert halo.shape == (nb, hist, D_POOL)
    assert cos.shape == (nt * tm, V7X_LANES)
    table_index = lambda b, t: (t, 0)
    seq_rows = nt * tm
    tail_total, tail_spec = _tail_spec(tail_rows, D_POOL, nt, tail_per_tile)
    qkv_shapes = [jax.ShapeDtypeStruct((nb, seq_rows // d, d * GROUP_W), BF16) for d in dils]
    qkv_specs = [pl.BlockSpec((None, tm // d, d * GROUP_W), lambda b, t: (b, t, 0)) for d in dils]
    vm = (2 * _nbytes((tm, D_MODEL), F32) + _nbytes((D_MODEL, QKV_W + D_POOL), BF16) + 2 * _nbytes((tm, QKV_W), BF16)
          + 2 * sum(_nbytes(blk, F32) for blk in win_blocks) + 4 * _nbytes((tm, D_POOL), F32)
          + _nbytes((2 * hist + tm, D_POOL), F32) + 2 * _nbytes((tail_rows, D_POOL), F32) + 8 * _nbytes((tm, A_WIDTH), F32))
    in_specs = [pl.BlockSpec((tm, D_MODEL), lambda b, t: (b * nt + t, 0)),
                _layer_block((1, D_MODEL), layer), _layer_block((D_MODEL, QKV_W + D_POOL), layer),
                pl.BlockSpec((tm, V7X_LANES), table_index), pl.BlockSpec((tm, V7X_LANES), table_index),
                _layer_block((len(POOL_WINDOWS), POOL_GC, POOL_GC), layer), _layer_block((1, D_POOL), layer),
                pl.BlockSpec((None, hist, D_POOL), lambda b, t: (b, 0, 0))]
    args = [x, nw, w_in, cos, sin, pool_w, pool_scale, halo]
    return pl.pallas_call(
        functools.partial(_ab_in_kernel, tm=tm, stride=stride, dils=dils, pos0=pos0, hist=hist, tail_rows=tail_rows,
                          win_first=win_first, win_chunks=win_chunks),
        grid=(nb, nt),
        in_specs=in_specs,
        out_specs=qkv_specs + [pl.BlockSpec((tm, D_POOL), lambda b, t: (b * nt + t, 0))]
                  + [pl.BlockSpec(blk, idx) for blk, idx in zip(win_blocks, win_index)] + [tail_spec],
        out_shape=qkv_shapes + [jax.ShapeDtypeStruct((rows, D_POOL), BF16)]
                  + [jax.ShapeDtypeStruct(s, F32) for s in win_shapes]
                  + [jax.ShapeDtypeStruct((nb, tail_total, D_POOL), F32)],
        scratch_shapes=[pltpu.VMEM((hist + tm, D_POOL), F32), pltpu.VMEM((LANE_CHUNKS, tm, V7X_LANES), F32)],
        compiler_params=_cparams(2, vm),
        name="ab_in",
    )(*args)


def _attn_kernel(*refs, side_seg, guest, grid_dims):
    refs = list(refs)
    own_in = refs[:5]
    del refs[:5]
    if side_seg:
        sq_ref, c0_ref, c1_ref, c2_ref = refs[:4]
        del refs[:4]
    if guest:
        guest_in = refs[:5]
        del refs[:5]
    out_ref, lse_ref = refs[:2]
    del refs[:2]
    if side_seg:
        _attn_s_body(sq_ref, (c0_ref, c1_ref, c2_ref), refs.pop(0), side_seg)
    n = pl.program_id(2)
    _band_attention(*own_in, out_ref, lse_ref, ATT_TQ // ATT_BLK, n > 0)
    if guest:
        first_block, seq_blocks = guest
        _, dil, nq = grid_dims
        block = first_block + (pl.program_id(0) * dil + pl.program_id(1)) * nq + n
        _band_attention(*guest_in, refs[0], refs[1], 1, block % seq_blocks != 0)


def _band_attention(q_ref, kc_ref, kp_ref, vc_ref, vp_ref, out_ref, lse_ref, nsub, has_prev):
    blk = ATT_BLK
    qi = lax.broadcasted_iota(jnp.int32, (blk, 2 * blk), 0)
    kj = lax.broadcasted_iota(jnp.int32, (blk, 2 * blk), 1)
    band = (kj >= qi) & (kj <= qi + SPAN)
    band_first = band & ((kj >= blk) | has_prev)
    lane = lax.broadcasted_iota(jnp.int32, (blk, V7X_LANES), 1)
    even_head = lane < HEAD_DIM
    lane_head = lane // LSE_LANES

    for c in range(nsub):
        rows = slice(c * blk, (c + 1) * blk)
        mask = band_first if c == 0 else band
        mask2 = jnp.concatenate([mask, mask], axis=0)
        lse_blk = jnp.zeros((blk, V7X_LANES), F32)
        for hp in range(A_HEADS // 2):
            ls = slice(hp * V7X_LANES, (hp + 1) * V7X_LANES)
            q2 = q_ref[rows, ls].astype(F32)
            qs = jnp.concatenate([jnp.where(even_head, q2, 0.0), jnp.where(even_head, 0.0, q2)], axis=0).astype(BF16)
            if c == 0:
                k2 = jnp.concatenate([kp_ref[:, ls], kc_ref[0:blk, ls]], axis=0)
                v2 = jnp.concatenate([vp_ref[:, ls], vc_ref[0:blk, ls]], axis=0)
            else:
                k2 = kc_ref[(c - 1) * blk:(c + 1) * blk, ls]
                v2 = vc_ref[(c - 1) * blk:(c + 1) * blk, ls]
            s = jnp.where(mask2, _dot_nt(qs, k2), NEG_INF)
            m = jnp.max(s, axis=-1, keepdims=True)
            p = jnp.exp(s - m)
            den = jnp.sum(p, axis=-1, keepdims=True)
            o = _dot(p.astype(BF16), v2) / den
            lse = m + jnp.log(den)
            out_ref[rows, ls] = jnp.where(even_head, o[0:blk], o[blk:2 * blk]).astype(BF16)
            lse_blk = jnp.where(lane_head == 2 * hp, lse[0:blk],
                                jnp.where(lane_head == 2 * hp + 1, lse[blk:2 * blk], lse_blk))
        lse_ref[rows, :] = lse_blk


def _attn_group(qkv_g, g, nb, seq, side=None, guest=None):
    dil = DIL_PATTERNS[g][1]
    sub = seq // dil
    tq, blk = ATT_TQ, ATT_BLK
    nq = sub // tq
    parts = GROUP_W // A_WIDTH
    cur = lambda part: pl.BlockSpec((None, tq, A_WIDTH), lambda b, r, n: (b, n, r * parts + part))
    prev = lambda part: pl.BlockSpec((None, blk, A_WIDTH),
                                     lambda b, r, n: (b, jnp.maximum(n * (tq // blk) - 1, 0), r * parts + part))
    in_specs = [cur(0), cur(1), prev(1), cur(2), prev(2)]
    args = [qkv_g] * 5
    o_spec = pl.BlockSpec((None, tq, A_WIDTH), lambda b, r, n: (b, n, r))
    l_spec = pl.BlockSpec((None, tq, V7X_LANES), lambda b, r, n: (b, n, r))
    out_specs = [o_spec, l_spec]
    out_shape = [jax.ShapeDtypeStruct((nb, sub, dil * A_WIDTH), BF16),
                 jax.ShapeDtypeStruct((nb, sub, dil * V7X_LANES), F32)]
    vm = 2 * (10 * _nbytes((tq, A_WIDTH), BF16) + 3 * _nbytes((tq, V7X_LANES), F32))
    side_seg = 0
    if side is not None:
        s_qkv, caches_t, jc, first_seq, n_seq, side_seg = side
        steps = nb * dil * nq
        n = n_seq // steps
        assert n * steps == n_seq and first_seq % n == 0 and first_seq + n_seq <= caches_t[0].shape[1]
        step = lambda b, r, i: (b * dil + r) * nq + i
        in_specs += ([pl.BlockSpec((n * side_seg, QKV_W), lambda b, r, i: (first_seq // n + step(b, r, i), 0))]
                     + [pl.BlockSpec((None, n, KV_W, c.shape[3]),
                                     lambda b, r, i: (jc, first_seq // n + step(b, r, i), 0, 0)) for c in caches_t])
        args += [s_qkv, *caches_t]
        out_specs.append(pl.BlockSpec((n * side_seg, A_WIDTH), lambda b, r, i: (step(b, r, i), 0)))
        out_shape.append(jax.ShapeDtypeStruct((n_seq * side_seg, A_WIDTH), F32))
        cache_bytes = n * sum(_nbytes(c.shape[2:], F32) for c in caches_t)
        vm += (2 * (cache_bytes + _nbytes((n * side_seg, QKV_W), F32))
               + 16 * _nbytes((A_HEADS * side_seg, ATT_S_CHUNK), F32) + 4 * _nbytes((A_WIDTH, ATT_S_CHUNK), F32))
    if guest is not None:
        g_qkv, first_block, seq_blocks = guest
        lin = lambda b, r, i: (b * dil + r) * nq + i
        g_cur = lambda part: pl.BlockSpec((blk, A_WIDTH), lambda b, r, i: (first_block + lin(b, r, i), part))
        g_prev = lambda part: pl.BlockSpec(
            (blk, A_WIDTH), lambda b, r, i: (jnp.maximum(first_block + lin(b, r, i) - 1, 0), part))
        in_specs += [g_cur(0), g_cur(1), g_prev(1), g_cur(2), g_prev(2)]
        args += [g_qkv] * 5
        n_blocks = nb * dil * nq
        out_specs += [pl.BlockSpec((blk, A_WIDTH), lambda b, r, i: (lin(b, r, i), 0)),
                      pl.BlockSpec((blk, V7X_LANES), lambda b, r, i: (lin(b, r, i), 0))]
        out_shape += [jax.ShapeDtypeStruct((n_blocks * blk, A_WIDTH), BF16),
                      jax.ShapeDtypeStruct((n_blocks * blk, V7X_LANES), F32)]
        vm += 2 * (6 * _nbytes((blk, A_WIDTH), BF16) + _nbytes((blk, V7X_LANES), F32))
        guest = (first_block, seq_blocks)
    res = pl.pallas_call(
        functools.partial(_attn_kernel, side_seg=side_seg, guest=guest, grid_dims=(nb, dil, nq)),
        grid=(nb, dil, nq),
        in_specs=in_specs,
        out_specs=out_specs,
        out_shape=out_shape,
        compiler_params=_cparams(3, vm),
        name=f"attn_d{dil}",
    )(*args)
    return tuple(res)


def _attn_s_body(qkv_ref, c_refs, out_ref, seg):
    hq = A_HEADS * seg
    row = lax.broadcasted_iota(jnp.int32, (hq, A_WIDTH), 0)
    lane = lax.broadcasted_iota(jnp.int32, (hq, A_WIDTH), 1)
    head_lanes = (lane // HEAD_DIM) == (row // seg)
    out_lane = lax.broadcasted_iota(jnp.int32, (seg, A_WIDTH), 1) // HEAD_DIM
    pad = V7X_LANES - seg
    for s in range(c_refs[0].shape[0]):
        new = qkv_ref[s * seg:(s + 1) * seg, :]
        scores, values = [], []
        for g, ((win, dil), c_ref) in enumerate(zip(DIL_PATTERNS, c_refs)):
            base = g * GROUP_W
            q = new[:, base:base + A_WIDTH]
            qrows = jnp.where(head_lanes, jnp.concatenate([q] * A_HEADS, axis=0), 0.0).astype(BF16)
            nk = c_ref.shape[2]
            ch = min(nk, ATT_S_CHUNK)
            tok = lax.broadcasted_iota(jnp.int32, (hq, ch), 0) % seg
            key = lax.broadcasted_iota(jnp.int32, (hq, ch), 1)
            same_class = (key % dil) == (tok % dil)
            for lo in range(0, nk, ch):
                valid = same_class & (key >= tok) if lo < seg else same_class
                kt = c_ref[s, 0:A_WIDTH, lo:lo + ch].astype(BF16)
                scores.append(jnp.where(valid, _dot(qrows, kt), NEG_INF))
                values.append((c_ref, lo, ch))
            kn = jnp.concatenate([new[:, base + A_WIDTH:base + 2 * A_WIDTH], jnp.zeros((pad, A_WIDTH), F32)], axis=0)
            vn = jnp.concatenate([new[:, base + 2 * A_WIDTH:base + 3 * A_WIDTH], jnp.zeros((pad, A_WIDTH), F32)], axis=0)
            tok_n = lax.broadcasted_iota(jnp.int32, (hq, V7X_LANES), 0) % seg
            key_n = lax.broadcasted_iota(jnp.int32, (hq, V7X_LANES), 1)
            valid_n = (key_n <= tok_n) & (((tok_n - key_n) % dil) == 0)
            scores.append(jnp.where(valid_n, _dot_nt(qrows, kn.astype(BF16)), NEG_INF))
            values.append(vn.astype(BF16))
        m = scores[0].max(axis=-1, keepdims=True)
        for sc in scores[1:]:
            m = jnp.maximum(m, sc.max(axis=-1, keepdims=True))
        den = jnp.zeros((hq, 1), F32)
        acc = jnp.zeros((hq, A_WIDTH), F32)
        for sc, val in zip(scores, values):
            p = jnp.exp(sc - m)
            den = den + p.sum(axis=-1, keepdims=True)
            if isinstance(val, tuple):
                c_ref, lo, ch = val
                acc = acc + _dot_nt(p.astype(BF16), c_ref[s, A_WIDTH:KV_W, lo:lo + ch].astype(BF16))
            else:
                acc = acc + _dot(p.astype(BF16), val)
        acc = acc / den
        o = jnp.zeros((seg, A_WIDTH), F32)
        for hh in range(A_HEADS):
            o = jnp.where(out_lane == hh, acc[hh * seg:(hh + 1) * seg], o)
        out_ref[s * seg:(s + 1) * seg, :] = o


def _rope_tables(pos):
    half = HEAD_DIM // 2
    inv = jnp.power(ROPE_THETA, -jnp.arange(half, dtype=F32) / half)
    reps = V7X_LANES // half
    ang = pos.astype(F32)[:, None] * jnp.tile(inv, reps)[None, :]
    sign = jnp.tile(jnp.concatenate([-jnp.ones((half,), F32), jnp.ones((half,), F32)]), reps // 2)
    return jnp.cos(ang), jnp.sin(ang) * sign[None, :]


def _prompt_mixers(w, nb, seq):
    tm = MIXER_ROW_TILE
    nt = seq // tm
    cos, sin = _rope_tables(jnp.arange(seq, dtype=jnp.int32))
    dils = tuple(d for _, d in DIL_PATTERNS)
    win_tot = tuple(min(win, seq) for win, _ in DIL_PATTERNS)
    win_r = tuple(min(wt, tm) for wt in win_tot)
    win_first = tuple(nt - max(wt // tm, 1) for wt in win_tot)
    win_cfg = dict(
        win_shapes=[(nb, KV_W, wt) for wt in win_tot],
        win_blocks=[(1, KV_W, r) for r in win_r],
        win_index=[(lambda b, t, first=first: (b, 0, jnp.maximum(t - first, 0))) for first in win_first],
        win_first=win_first,
        win_chunks=tuple(((0, tm - r, r),) for r in win_r))
    pool_halo = jnp.zeros((nb, _hist_rows(POOL_STATE, 1), D_POOL), F32)
    conv_halo = jnp.zeros((nb, _hist_rows(CONV_W - 1, 1), D_MODEL), F32)

    def ab_mixer(x, j, sample):
        q0, q1, q2, p, w0, w1, w2, ptail = _ab_in_call(
            x, w, j, cos, sin, pool_halo, nb=nb, nt=nt, tm=tm, stride=1, dils=dils, pos0=0,
            tail_rows=pool_halo.shape[1], tail_per_tile=False, **win_cfg)
        s_qkv, caches_t, nseq, seg = sample
        half = nseq // 2
        assert dils[0] == 1
        q0_rows = q0.reshape(nb * seq, GROUP_W)
        seq_blocks = seq // ATT_BLK
        half_blocks = nb * seq_blocks // 2
        o2, l2, att_lo, o0a, l0a = _attn_group(q2, 2, nb, seq, side=(s_qkv, caches_t, j, 0, half, seg),
                                               guest=(q0_rows, 0, seq_blocks))
        o1, l1, att_hi, o0b, l0b = _attn_group(q1, 1, nb, seq, side=(s_qkv, caches_t, j, half, nseq - half, seg),
                                               guest=(q0_rows, half_blocks, seq_blocks))
        o0 = jnp.concatenate([o0a, o0b], axis=0)
        l0 = jnp.concatenate([l0a, l0b], axis=0)
        groups = [(o0, l0, dils[0]), (o1, l1, dils[1]), (o2, l2, dils[2])]
        return (groups, p, (w0, w1, w2), ptail[:, ptail.shape[1] - POOL_STATE:],
                jnp.concatenate([att_lo, att_hi], axis=0))

    def conv_mixer(x, j):
        x, ctail = _conv_layer(x, w["mix_norm_c"], w["conv_w_in"], w["conv_w"], w["conv_w_out"], j, conv_halo,
                               nb=nb, nt=nt, tm=tm, stride=1, tail_rows=conv_halo.shape[1], tail_per_tile=False)
        return x, ctail[:, ctail.shape[1] - (CONV_W - 1):]

    def present_windows(wins):
        out = []
        for g in range(N_DIL):
            wt = jnp.stack(wins[g], axis=0).reshape(len(wins[g]), nb, 2, A_HEADS, HEAD_DIM, win_tot[g])
            out.append(jnp.transpose(wt, (0, 1, 5, 2, 3, 4)))
        return out

    return ab_mixer, conv_mixer, present_windows


def _ab_in_call(x, w, j, cos, sin, halo, **kw):
    return _ab_in(x, w["mix_norm_ab"], w["ab_w_in"], cos, sin, w["pool_w"], w["pool_scale"], j, halo, **kw)


def _sample_mixers(w, nseq, seg, pool_state, conv_state):
    tm = SAMPLE_ROW_TILE
    rows = seg * nseq
    nt = rows // tm
    tok_tile = tm // nseq
    cos, sin = _rope_tables(PAST_LEN + jnp.arange(seg, dtype=jnp.int32))
    cos, sin = jnp.repeat(cos, nseq, axis=0), jnp.repeat(sin, nseq, axis=0)
    win_cfg = dict(
        win_shapes=[(seg, KV_W, nseq)] * N_DIL,
        win_blocks=[(tok_tile, KV_W, nseq)] * N_DIL,
        win_index=[lambda b, t: (t, 0, 0)] * N_DIL,
        win_first=(0,) * N_DIL,
        win_chunks=(tuple((i, i * nseq, nseq) for i in range(tok_tile)),) * N_DIL)

    def ab_in(x, j):
        state = jnp.transpose(pool_state[j], (1, 0, 2))
        halo = state.reshape(1, POOL_STATE * nseq, D_POOL)
        q0, q1, q2, p, w0, w1, w2, u = _ab_in_call(
            x, w, j, cos, sin, halo, nb=1, nt=nt, tm=tm, stride=nseq, dils=(1,) * N_DIL, pos0=PAST_LEN,
            tail_rows=tm, tail_per_tile=True, **win_cfg)
        qkv = jnp.concatenate([q.reshape(seg, nseq, GROUP_W) for q in (q0, q1, q2)], axis=-1)
        qkv = jnp.transpose(qkv, (1, 0, 2)).reshape(rows, QKV_W).astype(F32)
        u = u.reshape(seg, nseq, D_POOL)
        pool = jnp.transpose(jnp.concatenate([state, u], axis=0)[-POOL_STATE:], (1, 0, 2))
        return qkv, p, (w0, w1, w2), pool

    def attn_rows(a):
        return jnp.transpose(a.reshape(nseq, seg, A_WIDTH), (1, 0, 2)).reshape(rows, A_WIDTH).astype(BF16)

    def conv_mixer(x, j):
        state = jnp.transpose(conv_state[j], (1, 0, 2))
        halo = state.reshape(1, (CONV_W - 1) * nseq, D_MODEL)
        x, cu = _conv_layer(x, w["mix_norm_c"], w["conv_w_in"], w["conv_w"], w["conv_w_out"], j, halo,
                            nb=1, nt=nt, tm=tm, stride=nseq, tail_rows=tm, tail_per_tile=True)
        cu = cu.reshape(seg, nseq, D_MODEL)
        return x, jnp.transpose(jnp.concatenate([state, cu], axis=0)[-(CONV_W - 1):], (1, 0, 2))

    def present_windows(wins):
        out = []
        for g in range(N_DIL):
            wt = jnp.stack(wins[g], axis=0).reshape(len(wins[g]), seg, 2, A_HEADS, HEAD_DIM, nseq)
            out.append(jnp.transpose(wt, (0, 5, 1, 2, 3, 4)))
        return out

    return ab_in, attn_rows, conv_mixer, present_windows


def _trunks(xp, xs, w, caches, pool_state, conv_state, *, nb, seq, nseq, seg):
    depth = w["ffn1_norm"].shape[0]
    p_ab, p_conv, p_windows = _prompt_mixers(w, nb, seq)
    s_ab_in, s_attn_rows, s_conv, s_windows = _sample_mixers(w, nseq, seg, pool_state, conv_state)
    caches_t = [jnp.transpose(c, (0, 1, 3, 4, 5, 2)).reshape(c.shape[0], nseq, KV_W, c.shape[2]) for c in caches]
    final_w = lambda l: w["final_norm"] if l == depth - 1 else None
    ffn1_both = lambda xa, xb, l: _ffn_pair(xa, xb, w["ffn1_norm"], w["ffn1_w_gu"], w["ffn1_w_down"], l)
    ffn2_both = lambda xa, xb, l: _ffn_pair(xa, xb, w["ffn2_norm"], w["ffn2_w_gu"], w["ffn2_w_down"], l,
                                            final_w=final_w(l))
    ffn2 = lambda x, l, **kw: _ffn(x, w["ffn2_norm"], w["ffn2_w_gu"], w["ffn2_w_down"], l, final_w=final_w(l), **kw)
    wins_p, wins_s = [[] for _ in range(N_DIL)], [[] for _ in range(N_DIL)]
    pools_p, pools_s, convs_p, convs_s = [], [], [], []
    for l in range(depth):
        j = l // 2
        xp, xs = ffn1_both(xp, xs, l)
        if l % 2 == 0:
            qkv_s, p_s, win_s, pool_s = s_ab_in(xs, j)
            a_p, p_p, win_p, pool_p, att_s = p_ab(xp, j, (qkv_s, caches_t, nseq, seg))
            xp = ffn2(xp, l, premix=(a_p, p_p, w["ab_w_out"], j))
            xs = ffn2(xs, l, premix=(s_attn_rows(att_s), p_s, w["ab_w_out"], j))
            for g in range(N_DIL):
                wins_p[g].append(win_p[g])
                wins_s[g].append(win_s[g])
            pools_p.append(pool_p)
            pools_s.append(pool_s)
        else:
            xp, conv_p = p_conv(xp, j)
            xs, conv_s = s_conv(xs, j)
            xp, xs = ffn2_both(xp, xs, l)
            convs_p.append(conv_p)
            convs_s.append(conv_s)
    return ((xp, p_windows(wins_p), jnp.stack(pools_p, axis=0), jnp.stack(convs_p, axis=0)),
            (xs, s_windows(wins_s), jnp.stack(pools_s, axis=0), jnp.stack(convs_s, axis=0)))


def kernel(x_prompt, x_sample, cache_win0, cache_win1, cache_win2, state_pool, state_conv, ffn1_norm, ffn1_w_gu, ffn1_w_down, mix_norm, ffn2_norm, ffn2_w_gu, ffn2_w_down, ab_w_in, ab_w_out, pool_w, pool_scale, conv_w_in, conv_w, conv_w_out, final_norm):
    batch, seq, _ = x_prompt.shape
    dbatch, dseq, _ = x_sample.shape
    assert seq % ROW_TILE == 0 and (dbatch * dseq) % ROW_TILE == 0 and dbatch % 2 == 0
    assert SAMPLE_ROW_TILE % dbatch == 0 and (dbatch * dseq) % SAMPLE_ROW_TILE == 0 and dbatch % V7X_SUBLANES == 0
    assert all((seq // dil) % ATT_TQ == 0 and ROW_TILE % dil == 0 for _, dil in DIL_PATTERNS)
    assert dseq == V7X_SUBLANES and PAST_LEN + 1 >= max(POOL_WINDOWS) and PAST_LEN >= POOL_STATE
    assert all(c.shape[2] == win and win == dil * SPAN for c, (win, dil) in zip((cache_win0, cache_win1, cache_win2), DIL_PATTERNS))
    row3 = lambda a: a.reshape(a.shape[0], 1, a.shape[1])
    w = dict(
        ffn1_norm=row3(ffn1_norm), ffn1_w_gu=ffn1_w_gu.astype(BF16), ffn1_w_down=ffn1_w_down.astype(BF16),
        ffn2_norm=row3(ffn2_norm), ffn2_w_gu=ffn2_w_gu.astype(BF16), ffn2_w_down=ffn2_w_down.astype(BF16),
        mix_norm_ab=row3(mix_norm[0::2]), mix_norm_c=row3(mix_norm[1::2]),
        ab_w_in=ab_w_in.astype(BF16), ab_w_out=ab_w_out.astype(BF16), pool_w=pool_w.astype(BF16), pool_scale=row3(pool_scale),
        conv_w_in=conv_w_in.astype(BF16), conv_w=conv_w, conv_w_out=conv_w_out.astype(BF16), final_norm=final_norm)
    xs = jnp.transpose(x_sample, (1, 0, 2)).reshape(dseq * dbatch, D_MODEL)
    (y_p, win_p, pool_p, conv_p), (y_s, win_s, pool_s, conv_s) = _trunks(
        x_prompt.reshape(batch * seq, D_MODEL), xs, w, (cache_win0, cache_win1, cache_win2), state_pool, state_conv,
        nb=batch, seq=seq, nseq=dbatch, seg=dseq)
    y_s = jnp.transpose(y_s.reshape(dseq, dbatch, D_MODEL), (1, 0, 2))
    return (y_p.reshape(batch, seq, D_MODEL), y_s,
            win_p[0], win_p[1], win_p[2], pool_p, conv_p,
            win_s[0], win_s[1], win_s[2], pool_s, conv_s)
```
